```python
import math
import jax, jax.numpy as jnp
from jax import lax
import numpy as np

D_MODEL = 1024
BATCH = 8
SEQ = 2048
DEPTH = 2
DEC_BATCH = 128
DEC_SEQ = 8
PAST_LEN = 2048
PAGE_SIZE = 128

H_A = 4
DH_A = 64
MOBA_BLOCK = 256
MOBA_TOPK = 3
MOBA_Q_CHUNK = 64
C_B = 256
CONV_W = 31
H_C = 4
DH_C = 64
DIFF_Q_BLOCK = 128
MEM_LEN = 256
XH = 4
XDH = D_MODEL // XH
N_EXPERTS = 32
TOP_K = 4
D_FF = D_MODEL
SWIGLU_LIMIT = 7.0
SWIGLU_ALPHA = 1.702
MOE_BLOCK = 128
MIX_A = H_A * DH_A
MIX_B = C_B
MIX_C = H_C * 2 * DH_C
MIX_W = MIX_A + MIX_B + MIX_C
N_BRANCH = 3
IN_W = 3 * MIX_A + 2 * C_B + 3 * MIX_C + N_BRANCH * D_MODEL
DN_ALPHA = (2 * DEPTH) ** 0.25
DN_BETA = (8 * DEPTH) ** -0.25
LN_EPS = 1e-5
POOL_EXTRA = 4

kernel_name = 'hybrid_moba_conformer_diffattn_moe_step'


def _layernorm(x, g, b):
    x32 = x.astype(jnp.float32)
    mu = jnp.mean(x32, axis=-1, keepdims=True)
    var = jnp.mean(jnp.square(x32 - mu), axis=-1, keepdims=True)
    return ((x32 - mu) * lax.rsqrt(var + LN_EPS) * g + b).astype(x.dtype)


def _alibi_slopes():
    n = H_A + H_C
    s = jnp.exp2(-8.0 * jnp.arange(1, n + 1, dtype=jnp.float32) / n)
    return s[n - H_A:], s[:n - H_A]


def _moba_attention(q, k, v, pos0, slopes):
    bsz, t_len, nh, dh = q.shape
    l_len = k.shape[1]
    nb = -(-l_len // MOBA_BLOCK)
    pad = nb * MOBA_BLOCK - l_len

    def to_blocks(a):
        a = jnp.pad(a, ((0, 0), (0, pad), (0, 0), (0, 0)))
        return a.reshape(bsz, nb, MOBA_BLOCK, nh, dh).transpose(0, 3, 1, 2, 4)

    kb, vb = to_blocks(k), to_blocks(v)
    n_sel = min(MOBA_TOPK, nb)
    cq = min(MOBA_Q_CHUNK, t_len)
    n_chunks = t_len // cq
    qc = q.reshape(bsz, n_chunks, cq, nh, dh).transpose(0, 1, 3, 2, 4)
    scale = dh ** -0.5
    blk_ids = jnp.arange(nb)
    offs = jnp.arange(MOBA_BLOCK)

    def per_seq(args):
        q_s, kb_s, vb_s = args
        k_mean = jnp.mean(kb_s.astype(jnp.float32), axis=2)

        def per_chunk(args_c):
            q_c, ci = args_c
            t = pos0 + ci * cq + jnp.arange(cq)
            cur = t // MOBA_BLOCK
            gate = jnp.einsum('hqd,hnd->hqn', q_c.astype(jnp.float32), k_mean)
            gate = jnp.where((blk_ids[None, :] < cur[:, None])[None], gate, -jnp.inf)
            g_val, g_idx = lax.top_k(gate, n_sel)
            slot_ok = jnp.isfinite(g_val)
            k_sel = jax.vmap(lambda kh, ih: kh[ih])(kb_s, g_idx)
            v_sel = jax.vmap(lambda vh, ih: vh[ih])(vb_s, g_idx)
            own = cur[0]
            k_own = lax.dynamic_index_in_dim(kb_s, own, axis=1, keepdims=False)
            v_own = lax.dynamic_index_in_dim(vb_s, own, axis=1, keepdims=False)
            pos_sel = (g_idx[..., None] * MOBA_BLOCK + offs).astype(jnp.float32)
            s_sel = jnp.einsum('hqd,hqnjd->hqnj', q_c, k_sel).astype(jnp.float32) * scale
            s_sel = s_sel - slopes[:, None, None, None] * (t.astype(jnp.float32)[None, :, None, None] - pos_sel)
            s_sel = jnp.where(slot_ok[..., None], s_sel, -jnp.inf)
            pos_own = own * MOBA_BLOCK + offs
            dist_own = t[:, None] - pos_own[None, :]
            s_own = jnp.einsum('hqd,hjd->hqj', q_c, k_own).astype(jnp.float32) * scale
            s_own = jnp.where(dist_own[None] >= 0,
                              s_own - slopes[:, None, None] * dist_own.astype(jnp.float32)[None],
                              -jnp.inf)
            s = jnp.concatenate([s_sel.reshape(nh, cq, n_sel * MOBA_BLOCK), s_own], axis=-1)
            p = jax.nn.softmax(s, axis=-1).astype(vb_s.dtype)
            p_sel = p[..., :n_sel * MOBA_BLOCK].reshape(nh, cq, n_sel, MOBA_BLOCK)
            p_own = p[..., n_sel * MOBA_BLOCK:]
            return (jnp.einsum('hqnj,hqnjd->hqd', p_sel, v_sel)
                    + jnp.einsum('hqj,hjd->hqd', p_own, v_own))

        return lax.map(per_chunk, (q_s, jnp.arange(n_chunks)))

    out = lax.map(per_seq, (qc, kb, vb))
    return out.transpose(0, 1, 3, 2, 4).reshape(bsz, t_len, nh * dh)


def _conv_module(u, hist, conv_w, conv_b, ln_g, ln_b):
    a, b = jnp.split(u, 2, axis=-1)
    z = a * jax.nn.sigmoid(b)
    zp = jnp.concatenate([hist, z], axis=1)
    y = lax.conv_general_dilated(zp, conv_w[:, None, :], window_strides=(1,), padding='VALID',
                                 dimension_numbers=('NWC', 'WIO', 'NWC'),
                                 feature_group_count=C_B) + conv_b
    y = _layernorm(y, ln_g, ln_b)
    return y * jax.nn.sigmoid(y), zp[:, zp.shape[1] - (CONV_W - 1):]


def _diff_attention(q, k, v, pos0, lam, lam_init, norm_g, slopes):
    bsz, t_len, nh, _, d = q.shape
    l_len = k.shape[1]
    qb = min(DIFF_Q_BLOCK, t_len)
    nqb = t_len // qb
    qs = q.reshape(bsz, nqb, qb, nh, 2, d).transpose(1, 0, 2, 3, 4, 5)
    kpos = jnp.arange(l_len)
    scale = d ** -0.5

    def block(args):
        q_b, bi = args
        t = pos0 + bi * qb + jnp.arange(qb)
        dist = t[:, None] - kpos[None, :]
        s = jnp.einsum('bqhmd,bkhmd->mbhqk', q_b, k).astype(jnp.float32) * scale
        s = s - slopes[None, None, :, None, None] * dist.astype(jnp.float32)
        s = jnp.where(dist >= 0, s, -jnp.inf)
        p = jax.nn.softmax(s, axis=-1)
        a = (p[0] - lam * p[1]).astype(v.dtype)
        o = jnp.einsum('bhqk,bkhe->bqhe', a, v).astype(jnp.float32)
        o = o * lax.rsqrt(jnp.mean(jnp.square(o), axis=-1, keepdims=True) + LN_EPS) * norm_g
        return (o * (1.0 - lam_init)).astype(v.dtype)

    out = lax.map(block, (qs, jnp.arange(nqb)))
    return out.transpose(1, 0, 2, 3, 4).reshape(bsz, t_len, nh * 2 * d)


def _cross_attention(x, mem_k, mem_v, wq, wo):
    bsz, t_len, _ = x.shape
    q = (x @ wq).reshape(bsz, t_len, XH, XDH)
    s = jnp.einsum('bthd,bmhd->bhtm', q, mem_k).astype(jnp.float32) * XDH ** -0.5
    p = jax.nn.softmax(s, axis=-1).astype(mem_v.dtype)
    o = jnp.einsum('bhtm,bmhd->bthd', p, mem_v).reshape(bsz, t_len, XH * XDH)
    return o @ wo


def _moe(x, router_w, router_b, w_gate_up, b_gate_up, w_down, b_down):
    bsz, t_len, d = x.shape
    n = bsz * t_len
    xf = x.reshape(n, d)
    logits = (xf @ router_w + router_b).astype(jnp.float32)
    top_v, top_i = lax.top_k(logits, TOP_K)
    gate = jax.nn.softmax(top_v, axis=-1)
    flat_e = top_i.reshape(-1)
    flat_t = jnp.repeat(jnp.arange(n, dtype=jnp.int32), TOP_K)
    flat_g = gate.reshape(-1)
    order = jnp.argsort(flat_e)
    se, st, sg = flat_e[order], flat_t[order], flat_g[order]
    counts = jnp.bincount(flat_e, length=N_EXPERTS)
    padded = (counts + MOE_BLOCK - 1) // MOE_BLOCK * MOE_BLOCK
    pad_end = jnp.cumsum(padded)
    start = jnp.cumsum(counts) - counts
    dest = (pad_end - padded)[se] + jnp.arange(n * TOP_K) - start[se]
    n_blocks = (n * TOP_K) // MOE_BLOCK + N_EXPERTS
    cap = n_blocks * MOE_BLOCK
    buf_t = jnp.zeros((cap,), jnp.int32).at[dest].set(st)
    buf_g = jnp.zeros((cap,), jnp.float32).at[dest].set(sg)
    blk_e = jnp.minimum(jnp.searchsorted(pad_end, jnp.arange(n_blocks) * MOE_BLOCK, side='right'),
                        N_EXPERTS - 1)
    xb = xf[buf_t].reshape(n_blocks, MOE_BLOCK, d)

    def expert_block(args):
        xe, e = args
        h = xe @ w_gate_up[e] + b_gate_up[e]
        hg = jnp.minimum(h[:, :D_FF], SWIGLU_LIMIT)
        hl = jnp.clip(h[:, D_FF:], -SWIGLU_LIMIT, SWIGLU_LIMIT)
        a = hg * jax.nn.sigmoid(SWIGLU_ALPHA * hg) * (hl + 1.0)
        return a @ w_down[e] + b_down[e]

    yb = lax.map(expert_block, (xb, blk_e))
    y = jnp.zeros((n, d), jnp.float32).at[buf_t].add(
        yb.reshape(cap, d).astype(jnp.float32) * buf_g[:, None])
    return y.astype(x.dtype).reshape(bsz, t_len, d)


def _decoder_layer(x, pos0, past_ka, past_va, past_kc, past_vc, conv_hist, mem_k, mem_v,
                   slopes_a, slopes_c, lam_init,
                   w_in, b_in, conv_w, conv_b, conv_ln_g, conv_ln_b,
                   lam_q1, lam_k1, lam_q2, lam_k2, diff_norm_g, w_branch, w_out, ln1_g, ln1_b,
                   xq_w, xo_w, ln2_g, ln2_b,
                   router_w, router_b, w_gate_up, b_gate_up, w_down, b_down, ln3_g, ln3_b):
    bsz, t_len, d = x.shape
    splits = [MIX_A, 2 * MIX_A, 3 * MIX_A, 3 * MIX_A + 2 * C_B,
              3 * MIX_A + 2 * C_B + MIX_C, 3 * MIX_A + 2 * C_B + 2 * MIX_C,
              3 * MIX_A + 2 * C_B + 3 * MIX_C]
    proj = x @ w_in + b_in
    qa, ka, va, u, qc, kc, vc, gl = jnp.split(proj, splits, axis=-1)
    ka = ka.reshape(bsz, t_len, H_A, DH_A)
    va = va.reshape(bsz, t_len, H_A, DH_A)
    ka_all = jnp.concatenate([past_ka, ka], axis=1)
    va_all = jnp.concatenate([past_va, va], axis=1)
    o_a = _moba_attention(qa.reshape(bsz, t_len, H_A, DH_A), ka_all, va_all, pos0, slopes_a)
    o_b, conv_state = _conv_module(u, conv_hist, conv_w, conv_b, conv_ln_g, conv_ln_b)
    kc = kc.reshape(bsz, t_len, H_C, 2 * DH_C)
    vc = vc.reshape(bsz, t_len, H_C, 2 * DH_C)
    kc_all = jnp.concatenate([past_kc, kc], axis=1)
    vc_all = jnp.concatenate([past_vc, vc], axis=1)
    lam = (jnp.exp(jnp.sum(lam_q1.astype(jnp.float32) * lam_k1.astype(jnp.float32)))
           - jnp.exp(jnp.sum(lam_q2.astype(jnp.float32) * lam_k2.astype(jnp.float32)))
           + lam_init)
    o_c = _diff_attention(qc.reshape(bsz, t_len, H_C, 2, DH_C),
                          kc_all.reshape(bsz, kc_all.shape[1], H_C, 2, DH_C),
                          vc_all, pos0, lam, lam_init, diff_norm_g, slopes_c)
    g = jax.nn.sigmoid(gl).reshape(bsz, t_len, N_BRANCH, d)
    wa, wb, wc = jnp.split(w_branch, [MIX_A, MIX_A + MIX_B], axis=0)
    merged = g[:, :, 0] * (o_a @ wa) + g[:, :, 1] * (o_b @ wb) + g[:, :, 2] * (o_c @ wc)
    h = _layernorm(DN_ALPHA * x + merged @ w_out, ln1_g, ln1_b)
    h = _layernorm(DN_ALPHA * h + _cross_attention(h, mem_k, mem_v, xq_w, xo_w), ln2_g, ln2_b)
    h = _layernorm(DN_ALPHA * h + _moe(h, router_w, router_b, w_gate_up, b_gate_up, w_down, b_down),
                   ln3_g, ln3_b)
    return h, ka, va, kc, vc, conv_state


def setup_inputs(seed: int = 0) -> dict:
    key = jax.random.key(seed)
    ks = iter(jax.random.split(key, 64))

    def nrm(shape, scale):
        return jax.random.normal(next(ks), shape, jnp.float32) * scale

    n_pages = PAST_LEN // PAGE_SIZE
    n_used = DEC_BATCH * n_pages
    n_pool = n_used + n_used // POOL_EXTRA
    page_table = jax.random.permutation(next(ks), n_pool)[:n_used].reshape(DEC_BATCH, n_pages).astype(jnp.int32)
    return {
        'x_prompt': nrm((BATCH, SEQ, D_MODEL), 1.0),
        'x_sample': nrm((DEC_BATCH, DEC_SEQ, D_MODEL), 1.0),
        'mem_prompt': nrm((BATCH, MEM_LEN, D_MODEL), 1.0),
        'cache_moba_k': nrm((DEPTH, n_pool, PAGE_SIZE, H_A, DH_A), 1.0),
        'cache_moba_v': nrm((DEPTH, n_pool, PAGE_SIZE, H_A, DH_A), 1.0),
        'cache_diff_k': nrm((DEPTH, n_pool, PAGE_SIZE, H_C, 2 * DH_C), 1.0),
        'cache_diff_v': nrm((DEPTH, n_pool, PAGE_SIZE, H_C, 2 * DH_C), 1.0),
        'cache_mem_k': nrm((DEPTH, DEC_BATCH, MEM_LEN, XH, XDH), 1.0),
        'cache_mem_v': nrm((DEPTH, DEC_BATCH, MEM_LEN, XH, XDH), 1.0),
        'state_conv': nrm((DEPTH, DEC_BATCH, CONV_W - 1, C_B), 1.0),
        'page_table': page_table,
        'w_in': nrm((DEPTH, D_MODEL, IN_W), D_MODEL ** -0.5),
        'b_in': nrm((DEPTH, IN_W), 0.01),
        'conv_w': nrm((DEPTH, CONV_W, C_B), CONV_W ** -0.5),
        'conv_b': nrm((DEPTH, C_B), 0.01),
        'conv_ln_g': 1.0 + nrm((DEPTH, C_B), 0.01),
        'conv_ln_b': nrm((DEPTH, C_B), 0.01),
        'lam_q1': nrm((DEPTH, DH_C), 0.1),
        'lam_k1': nrm((DEPTH, DH_C), 0.1),
        'lam_q2': nrm((DEPTH, DH_C), 0.1),
        'lam_k2': nrm((DEPTH, DH_C), 0.1),
        'diff_norm_g': 1.0 + nrm((DEPTH, 2 * DH_C), 0.01),
        'w_branch': nrm((DEPTH, MIX_W, D_MODEL), (MIX_W / N_BRANCH) ** -0.5),
        'w_out': nrm((DEPTH, D_MODEL, D_MODEL), D_MODEL ** -0.5 * DN_BETA),
        'ln1_g': 1.0 + nrm((DEPTH, D_MODEL), 0.01),
        'ln1_b': nrm((DEPTH, D_MODEL), 0.01),
        'xq_w': nrm((DEPTH, D_MODEL, XH * XDH), D_MODEL ** -0.5),
        'xk_w': nrm((DEPTH, D_MODEL, XH * XDH), D_MODEL ** -0.5),
        'xv_w': nrm((DEPTH, D_MODEL, XH * XDH), D_MODEL ** -0.5),
        'xo_w': nrm((DEPTH, XH * XDH, D_MODEL), (XH * XDH) ** -0.5 * DN_BETA),
        'ln2_g': 1.0 + nrm((DEPTH, D_MODEL), 0.01),
        'ln2_b': nrm((DEPTH, D_MODEL), 0.01),
        'router_w': nrm((DEPTH, D_MODEL, N_EXPERTS), D_MODEL ** -0.5),
        'router_b': nrm((DEPTH, N_EXPERTS), 0.01),
        'w_gate_up': nrm((DEPTH, N_EXPERTS, D_MODEL, 2 * D_FF), D_MODEL ** -0.5),
        'b_gate_up': nrm((DEPTH, N_EXPERTS, 2 * D_FF), 0.01),
        'w_down': nrm((DEPTH, N_EXPERTS, D_FF, D_MODEL), D_FF ** -0.5 * DN_BETA),
        'b_down': nrm((DEPTH, N_EXPERTS, D_MODEL), 0.01),
        'ln3_g': 1.0 + nrm((DEPTH, D_MODEL), 0.01),
        'ln3_b': nrm((DEPTH, D_MODEL), 0.01),
    }


def reference(x_prompt, x_sample, mem_prompt, cache_moba_k, cache_moba_v, cache_diff_k, cache_diff_v,
              cache_mem_k, cache_mem_v, state_conv, page_table,
              w_in, b_in, conv_w, conv_b, conv_ln_g, conv_ln_b, lam_q1, lam_k1, lam_q2, lam_k2,
              diff_norm_g, w_branch, w_out, ln1_g, ln1_b, xq_w, xk_w, xv_w, xo_w, ln2_g, ln2_b,
              router_w, router_b, w_gate_up, b_gate_up, w_down, b_down, ln3_g, ln3_b):
    n_pages = PAST_LEN // PAGE_SIZE
    bp = x_prompt.shape[0]
    db = x_sample.shape[0]
    slopes_a, slopes_c = _alibi_slopes()
    hp, hs = x_prompt, x_sample
    mk_p_l, mv_p_l, ka_p_l, va_p_l, kc_p_l, vc_p_l, cs_p_l = [], [], [], [], [], [], []
    ka_s_l, va_s_l, kc_s_l, vc_s_l, cs_s_l = [], [], [], [], []

    def gather_pages(pool, l):
        return pool[l][page_table].reshape(db, n_pages * PAGE_SIZE, pool.shape[3], pool.shape[4])

    for l in range(DEPTH):
        lam_init = 0.8 - 0.6 * math.exp(-0.3 * l)
        lw = (w_in[l], b_in[l], conv_w[l], conv_b[l], conv_ln_g[l], conv_ln_b[l],
              lam_q1[l], lam_k1[l], lam_q2[l], lam_k2[l], diff_norm_g[l], w_branch[l], w_out[l],
              ln1_g[l], ln1_b[l], xq_w[l], xo_w[l], ln2_g[l], ln2_b[l],
              router_w[l], router_b[l], w_gate_up[l], b_gate_up[l], w_down[l], b_down[l],
              ln3_g[l], ln3_b[l])
        mk_p = (mem_prompt @ xk_w[l]).reshape(bp, mem_prompt.shape[1], XH, XDH)
        mv_p = (mem_prompt @ xv_w[l]).reshape(bp, mem_prompt.shape[1], XH, XDH)
        empty_a = jnp.zeros((bp, 0, H_A, DH_A), hp.dtype)
        empty_c = jnp.zeros((bp, 0, H_C, 2 * DH_C), hp.dtype)
        hist0 = jnp.zeros((bp, CONV_W - 1, C_B), hp.dtype)
        hp, ka_p, va_p, kc_p, vc_p, cs_p = _decoder_layer(
            hp, 0, empty_a, empty_a, empty_c, empty_c, hist0, mk_p, mv_p,
            slopes_a, slopes_c, lam_init, *lw)
        hs, ka_s, va_s, kc_s, vc_s, cs_s = _decoder_layer(
            hs, n_pages * PAGE_SIZE,
            gather_pages(cache_moba_k, l), gather_pages(cache_moba_v, l),
            gather_pages(cache_diff_k, l), gather_pages(cache_diff_v, l),
            state_conv[l], cache_mem_k[l], cache_mem_v[l],
            slopes_a, slopes_c, lam_init, *lw)
        mk_p_l.append(mk_p); mv_p_l.append(mv_p)
        ka_p_l.append(ka_p); va_p_l.append(va_p); kc_p_l.append(kc_p); vc_p_l.append(vc_p); cs_p_l.append(cs_p)
        ka_s_l.append(ka_s); va_s_l.append(va_s); kc_s_l.append(kc_s); vc_s_l.append(vc_s); cs_s_l.append(cs_s)

    return (hp, hs,
            jnp.stack(ka_p_l), jnp.stack(va_p_l), jnp.stack(kc_p_l), jnp.stack(vc_p_l),
            jnp.stack(mk_p_l), jnp.stack(mv_p_l), jnp.stack(cs_p_l),
            jnp.stack(ka_s_l), jnp.stack(va_s_l), jnp.stack(kc_s_l), jnp.stack(vc_s_l),
            jnp.stack(cs_s_l))
```

```python
import functools
import math

import jax
import jax.numpy as jnp
from jax import lax
from jax.experimental import pallas as pl
from jax.experimental.pallas import tpu as pltpu

D_MODEL = 1024
DEPTH = 2
PAST_LEN = 2048
PAGE_SIZE = 128
N_PAGES = PAST_LEN // PAGE_SIZE

H_A = 4
DH_A = 64
MOBA_BLOCK = 256
MOBA_TOPK = 3
C_B = 256
CONV_W = 31
H_C = 4
DH_C = 64
MEM_LEN = 256
XH = 4
XDH = D_MODEL // XH
N_EXPERTS = 32
TOP_K = 4
D_FF = D_MODEL
SWIGLU_LIMIT = 7.0
SWIGLU_ALPHA = 1.702
MIX_A = H_A * DH_A
MIX_C = H_C * 2 * DH_C
N_BRANCH = 3
DN_ALPHA = (2 * DEPTH) ** 0.25
LN_EPS = 1e-5

SLOPES_C = tuple(2.0 ** -(i + 1) for i in range(0, 4))
SLOPES_A = tuple(2.0 ** -(i + 1) for i in range(4, 8))

COL_G = 0
COL_QC = 3 * D_MODEL
COL_KC = COL_QC + MIX_C
COL_VC = COL_KC + MIX_C
COL_QA = COL_VC + MIX_C
COL_KA = COL_QA + MIX_A
COL_VA = COL_KA + MIX_A
COL_UA = COL_VA + MIX_A
COL_UB = COL_UA + C_B
IN_W = COL_UB + C_B

LANES = 128
Q_TILE = 128
MOE_BLOCK = 256
VMEM_LIMIT = 56 * 1024 * 1024

_NT = (((1,), (1,)), ((), ()))
_NEG = -jnp.inf
BF = jnp.bfloat16
F32 = jnp.float32


def _params(n_grid):
    return pltpu.CompilerParams(dimension_semantics=("arbitrary",) * n_grid,
                                vmem_limit_bytes=VMEM_LIMIT)


def _sigmoid(x):
    return 1.0 / (1.0 + jnp.exp(-x))


def _layernorm(x, g, b):
    mu = jnp.mean(x, axis=-1, keepdims=True)
    xc = x - mu
    var = jnp.mean(xc * xc, axis=-1, keepdims=True)
    return xc * lax.rsqrt(var + LN_EPS) * g + b


def _top_picks(g, lane, k):
    lane = lane.astype(F32)
    sel = jnp.zeros(g.shape, F32)
    vals, idxs = [], []
    for _ in range(k):
        m = jnp.max(g, axis=-1, keepdims=True)
        cand = jnp.where(g == m, lane, float(g.shape[-1]))
        idx = jnp.where(m > _NEG, jnp.min(cand, axis=-1, keepdims=True), -1.0)
        pick = lane == idx
        sel = jnp.where(pick, 1.0, sel)
        g = jnp.where(pick, _NEG, g)
        vals.append(m)
        idxs.append(idx)
    return vals, idxs, sel


def _mm_kernel(x_ref, w_ref, *rest, has_bias):
    o_ref = rest[-1]
    acc = jnp.dot(x_ref[...].astype(BF), w_ref[...], preferred_element_type=F32)
    if has_bias:
        acc = acc + rest[0][...]
    o_ref[...] = acc


def _matmul(x, w, b=None, *, tm, tn, name):
    m, k = x.shape
    n = w.shape[1]
    in_specs = [pl.BlockSpec((tm, k), lambda i, j: (i, 0)),
                pl.BlockSpec((k, tn), lambda i, j: (0, j))]
    args = [x, w]
    if b is not None:
        in_specs.append(pl.BlockSpec((1, tn), lambda i, j: (0, j)))
        args.append(b.reshape(1, n))
    return pl.pallas_call(
        functools.partial(_mm_kernel, has_bias=b is not None),
        grid=(pl.cdiv(m, tm), n // tn),
        in_specs=in_specs,
        out_specs=pl.BlockSpec((tm, tn), lambda i, j: (i, j)),
        out_shape=jax.ShapeDtypeStruct((m, n), F32),
        compiler_params=_params(2),
        name=name,
    )(*args)


def _mm_res_ln_kernel(x_ref, w_ref, r_ref, g_ref, b_ref, o_ref):
    y = jnp.dot(x_ref[...].astype(BF), w_ref[...], preferred_element_type=F32)
    o_ref[...] = _layernorm(DN_ALPHA * r_ref[...] + y, g_ref[...], b_ref[...])


def _mm_res_ln(x, w, res, g, b, *, tm, name):
    m, k = x.shape
    n = w.shape[1]
    return pl.pallas_call(
        _mm_res_ln_kernel,
        grid=(pl.cdiv(m, tm),),
        in_specs=[pl.BlockSpec((tm, k), lambda i: (i, 0)),
                  pl.BlockSpec((k, n), lambda i: (0, 0)),
                  pl.BlockSpec((tm, n), lambda i: (i, 0)),
                  pl.BlockSpec((1, n), lambda i: (0, 0)),
                  pl.BlockSpec((1, n), lambda i: (0, 0))],
        out_specs=pl.BlockSpec((tm, n), lambda i: (i, 0)),
        out_shape=jax.ShapeDtypeStruct((m, n), F32),
        compiler_params=_params(1),
        name=name,
    )(x, w, res, g.reshape(1, n), b.reshape(1, n))


def _res_ln_kernel(y_ref, r_ref, g_ref, b_ref, o_ref):
    o_ref[...] = _layernorm(DN_ALPHA * r_ref[...] + y_ref[...], g_ref[...], b_ref[...])


def _res_ln(y, res, g, b, *, tm, name):
    m, n = y.shape
    return pl.pallas_call(
        _res_ln_kernel,
        grid=(pl.cdiv(m, tm),),
        in_specs=[pl.BlockSpec((tm, n), lambda i: (i, 0)),
                  pl.BlockSpec((tm, n), lambda i: (i, 0)),
                  pl.BlockSpec((1, n), lambda i: (0, 0)),
                  pl.BlockSpec((1, n), lambda i: (0, 0))],
        out_specs=pl.BlockSpec((tm, n), lambda i: (i, 0)),
        out_shape=jax.ShapeDtypeStruct((m, n), F32),
        compiler_params=_params(1),
        name=name,
    )(y, res, g.reshape(1, n), b.reshape(1, n))


def _moba_prompt_kernel(q_ref, k_ref, v_ref, o_ref, kb_ref, vb_ref, km_ref, *, n_blocks):
    qi = pl.program_id(1)

    @pl.when(qi == 0)
    def _():
        kb_ref[...] = k_ref[...].astype(BF)
        vb_ref[...] = v_ref[...].astype(BF)
        km_ref[...] = jnp.zeros(km_ref.shape, F32)
        for n in range(n_blocks):
            km_ref[n:n + 1, :] = jnp.mean(k_ref[n * MOBA_BLOCK:(n + 1) * MOBA_BLOCK, :],
                                          axis=0, keepdims=True)

    tiles_per_block = MOBA_BLOCK // Q_TILE
    cur = qi // tiles_per_block
    q_off = (qi % tiles_per_block) * Q_TILE
    q = q_ref[...] * (DH_A ** -0.5)
    lane_q = lax.broadcasted_iota(jnp.int32, q.shape, 1)
    lane_g = lax.broadcasted_iota(jnp.int32, (Q_TILE, LANES), 1)
    dist0 = (q_off + lax.broadcasted_iota(jnp.int32, (Q_TILE, MOBA_BLOCK), 0)
             - lax.broadcasted_iota(jnp.int32, (Q_TILE, MOBA_BLOCK), 1))
    dist0_f = dist0.astype(F32)
    own_start = pl.multiple_of(cur * MOBA_BLOCK, MOBA_BLOCK)

    for pair in range(H_A // 2):
        cols = slice(pair * LANES, (pair + 1) * LANES)
        outs = []
        for hh in range(2):
            h = pair * 2 + hh
            slope = SLOPES_A[h]
            qm = jnp.where(lane_q // DH_A == h, q, 0.0)
            gate = lax.dot_general(qm, km_ref[...], _NT, precision=lax.Precision.HIGHEST,
                                   preferred_element_type=F32)
            gate = jnp.where(lane_g < cur, gate, _NEG)
            _, _, sel = _top_picks(gate, lane_g, MOBA_TOPK)
            qmb = qm[:, cols].astype(BF)

            s = lax.dot_general(qmb, kb_ref[pl.ds(own_start, MOBA_BLOCK), cols], _NT,
                                preferred_element_type=F32)
            s = jnp.where(dist0 >= 0, s - slope * dist0_f, _NEG)
            m0 = jnp.max(s, axis=-1, keepdims=True)
            p = jnp.exp(s - m0)
            l0 = jnp.sum(p, axis=-1, keepdims=True)
            acc0 = jnp.dot(p.astype(BF), vb_ref[pl.ds(own_start, MOBA_BLOCK), cols],
                           preferred_element_type=F32)

            def body(n, carry, qmb=qmb, sel=sel, slope=slope, cols=cols):
                m, l, acc = carry
                start = pl.multiple_of(n * MOBA_BLOCK, MOBA_BLOCK)
                s = lax.dot_general(qmb, kb_ref[pl.ds(start, MOBA_BLOCK), cols], _NT,
                                    preferred_element_type=F32)
                s = s - slope * (dist0 + (cur - n) * MOBA_BLOCK).astype(F32)
                picked = jnp.max(jnp.where(lane_g == n, sel, 0.0), axis=-1, keepdims=True) > 0.5
                s = jnp.where(picked, s, _NEG)
                m_new = jnp.maximum(m, jnp.max(s, axis=-1, keepdims=True))
                a = jnp.exp(m - m_new)
                p = jnp.exp(s - m_new)
                l = a * l + jnp.sum(p, axis=-1, keepdims=True)
                acc = a * acc + jnp.dot(p.astype(BF), vb_ref[pl.ds(start, MOBA_BLOCK), cols],
                                        preferred_element_type=F32)
                return m_new, l, acc

            _, l, acc = lax.fori_loop(0, cur, body, (m0, l0, acc0))
            outs.append(acc / l)
        o_ref[:, cols] = jnp.where(lane_g < DH_A, outs[0], outs[1])


def _moba_prompt(proj, batch, seq):
    n_q = seq // Q_TILE
    cq, ck, cv = COL_QA // MIX_A, COL_KA // MIX_A, COL_VA // MIX_A
    return pl.pallas_call(
        functools.partial(_moba_prompt_kernel, n_blocks=seq // MOBA_BLOCK),
        grid=(batch, n_q),
        in_specs=[pl.BlockSpec((Q_TILE, MIX_A), lambda b, i: (b * n_q + i, cq)),
                  pl.BlockSpec((seq, MIX_A), lambda b, i: (b, ck)),
                  pl.BlockSpec((seq, MIX_A), lambda b, i: (b, cv))],
        out_specs=pl.BlockSpec((Q_TILE, MIX_A), lambda b, i: (b * n_q + i, 0)),
        out_shape=jax.ShapeDtypeStruct((batch * seq, MIX_A), F32),
        scratch_shapes=[pltpu.VMEM((seq, MIX_A), BF), pltpu.VMEM((seq, MIX_A), BF),
                        pltpu.VMEM((LANES, MIX_A), F32)],
        compiler_params=_params(2),
        name="moba_prompt",
    )(proj, proj, proj)


def _lambda(lq1, lk1, lq2, lk2, lam_init):
    return (jnp.exp(jnp.sum(lq1[...] * lk1[...], axis=-1, keepdims=True))
            - jnp.exp(jnp.sum(lq2[...] * lk2[...], axis=-1, keepdims=True)) + lam_init)


def _diff_prompt_kernel(q_ref, k_ref, v_ref, lq1, lk1, lq2, lk2, ng_ref, o_ref, kb_ref, vb_ref,
                        *, lam_init):
    qi = pl.program_id(1)

    @pl.when(qi == 0)
    def _():
        kb_ref[...] = k_ref[...].astype(BF)
        vb_ref[...] = v_ref[...].astype(BF)

    kblk = MOBA_BLOCK
    tiles_per_block = kblk // Q_TILE
    cur = qi // tiles_per_block
    q_off = (qi % tiles_per_block) * Q_TILE
    lam = _lambda(lq1, lk1, lq2, lk2, lam_init)
    lane = lax.broadcasted_iota(jnp.int32, (Q_TILE, LANES), 1)
    dist0 = (q_off + lax.broadcasted_iota(jnp.int32, (Q_TILE, kblk), 0)
             - lax.broadcasted_iota(jnp.int32, (Q_TILE, kblk), 1))
    dist0_f = dist0.astype(F32)
    own_start = pl.multiple_of(cur * kblk, kblk)

    def start_state(qb, kb, vb, slope):
        s = lax.dot_general(qb, kb, _NT, preferred_element_type=F32)
        s = jnp.where(dist0 >= 0, s - slope * dist0_f, _NEG)
        m = jnp.max(s, axis=-1, keepdims=True)
        p = jnp.exp(s - m)
        return m, jnp.sum(p, axis=-1, keepdims=True), jnp.dot(p.astype(BF), vb,
                                                              preferred_element_type=F32)

    def step_state(state, qb, kb, vb, bias):
        m, l, acc = state
        s = lax.dot_general(qb, kb, _NT, preferred_element_type=F32) - bias
        m_new = jnp.maximum(m, jnp.max(s, axis=-1, keepdims=True))
        a = jnp.exp(m - m_new)
        p = jnp.exp(s - m_new)
        return (m_new, a * l + jnp.sum(p, axis=-1, keepdims=True),
                a * acc + jnp.dot(p.astype(BF), vb, preferred_element_type=F32))

    for h in range(H_C):
        cols = slice(h * LANES, (h + 1) * LANES)
        slope = SLOPES_C[h]
        qh = q_ref[:, cols] * (DH_C ** -0.5)
        q1 = jnp.where(lane < DH_C, qh, 0.0).astype(BF)
        q2 = jnp.where(lane >= DH_C, qh, 0.0).astype(BF)
        kb = kb_ref[pl.ds(own_start, kblk), cols]
        vb = vb_ref[pl.ds(own_start, kblk), cols]
        st1 = start_state(q1, kb, vb, slope)
        st2 = start_state(q2, kb, vb, slope)

        def body(n, carry, q1=q1, q2=q2, slope=slope, cols=cols):
            st1, st2 = carry
            start = pl.multiple_of(n * kblk, kblk)
            kb = kb_ref[pl.ds(start, kblk), cols]
            vb = vb_ref[pl.ds(start, kblk), cols]
            bias = slope * (dist0 + (cur - n) * kblk).astype(F32)
            return step_state(st1, q1, kb, vb, bias), step_state(st2, q2, kb, vb, bias)

        (_, l1, a1), (_, l2, a2) = lax.fori_loop(0, cur, body, (st1, st2))
        o = a1 / l1 - lam * (a2 / l2)
        o = o * lax.rsqrt(jnp.mean(o * o, axis=-1, keepdims=True) + LN_EPS) * ng_ref[...]
        o_ref[:, cols] = o * (1.0 - lam_init)


def _diff_prompt(proj, lam_vecs, norm_g, batch, seq, lam_init):
    n_q = seq // Q_TILE
    cq, ck, cv = COL_QC // MIX_C, COL_KC // MIX_C, COL_VC // MIX_C
    vec = pl.BlockSpec((1, DH_C), lambda b, i: (0, 0))
    return pl.pallas_call(
        functools.partial(_diff_prompt_kernel, lam_init=lam_init),
        grid=(batch, n_q),
        in_specs=[pl.BlockSpec((Q_TILE, MIX_C), lambda b, i: (b * n_q + i, cq)),
                  pl.BlockSpec((seq, MIX_C), lambda b, i: (b, ck)),
                  pl.BlockSpec((seq, MIX_C), lambda b, i: (b, cv)),
                  vec, vec, vec, vec,
                  pl.BlockSpec((1, 2 * DH_C), lambda b, i: (0, 0))],
        out_specs=pl.BlockSpec((Q_TILE, MIX_C), lambda b, i: (b * n_q + i, 0)),
        out_shape=jax.ShapeDtypeStruct((batch * seq, MIX_C), F32),
        scratch_shapes=[pltpu.VMEM((seq, MIX_C), BF), pltpu.VMEM((seq, MIX_C), BF)],
        compiler_params=_params(2),
        name="diff_prompt",
    )(proj, proj, proj, *lam_vecs, norm_g.reshape(1, 2 * DH_C))


def _sample_attn_kernel(pt_ref, qa_ref, ka_ref, va_ref, qc_ref, kc_ref, vc_ref,
                        lq1, lk1, lq2, lk2, ng_ref, *rest, lam_init, t_new):
    del pt_ref
    n = N_PAGES
    mk, mv = rest[0:n], rest[n:2 * n]
    dk, dv = rest[2 * n:3 * n], rest[3 * n:4 * n]
    oa_ref, oc_ref = rest[4 * n], rest[4 * n + 1]
    pad_rows = PAGE_SIZE - t_new

    def pad_new(x):
        return jnp.concatenate([x, jnp.zeros((pad_rows, x.shape[1]), x.dtype)], axis=0)

    def attend(q_rows, k_pages, k_new, row_slope, row_t, gate_sel):
        r = q_rows.shape[0]
        qb = q_rows.astype(BF)
        lane = lax.broadcasted_iota(jnp.int32, (r, PAGE_SIZE), 1)
        lane_f = lane.astype(F32)
        tiles = []
        for j in range(n):
            s = lax.dot_general(qb, k_pages[j][...].astype(BF), _NT, preferred_element_type=F32)
            s = s - row_slope * (float(PAST_LEN - j * PAGE_SIZE) + row_t - lane_f)
            if gate_sel is not None:
                s = jnp.where(gate_sel[j // (MOBA_BLOCK // PAGE_SIZE)], s, _NEG)
            tiles.append(s)
        s = lax.dot_general(qb, pad_new(k_new).astype(BF), _NT, preferred_element_type=F32)
        s = jnp.where(lane_f <= row_t, s - row_slope * (row_t - lane_f), _NEG)
        tiles.append(s)
        m = tiles[0].max(axis=-1, keepdims=True)
        for s in tiles[1:]:
            m = jnp.maximum(m, s.max(axis=-1, keepdims=True))
        ps = [jnp.exp(s - m) for s in tiles]
        l = ps[0].sum(axis=-1, keepdims=True)
        for p in ps[1:]:
            l = l + p.sum(axis=-1, keepdims=True)
        return ps, l

    def weighted_values(ws, v_pages, v_new):
        acc = jnp.dot(ws[n].astype(BF), pad_new(v_new).astype(BF), preferred_element_type=F32)
        for j in range(n):
            acc = acc + jnp.dot(ws[j].astype(BF), v_pages[j][...].astype(BF),
                                preferred_element_type=F32)
        return acc

    def fold_heads(o, head_w, n_heads):
        r = lax.broadcasted_iota(jnp.int32, o.shape, 0)
        c = lax.broadcasted_iota(jnp.int32, o.shape, 1)
        o = jnp.where(r // t_new == c // head_w, o, 0.0)
        out = o[0:t_new]
        for h in range(1, n_heads):
            out = out + o[h * t_new:(h + 1) * t_new]
        return out

    def row_const(rows, values, group):
        h = lax.broadcasted_iota(jnp.int32, (rows, 1), 0) // group
        out = jnp.full((rows, 1), values[-1], F32)
        for i in range(len(values) - 2, -1, -1):
            out = jnp.where(h % len(values) == i, values[i], out)
        return out

    ra = H_A * t_new
    qa = qa_ref[...] * (DH_A ** -0.5)
    qa_rows = jnp.concatenate([qa] * H_A, axis=0)
    r = lax.broadcasted_iota(jnp.int32, qa_rows.shape, 0)
    c = lax.broadcasted_iota(jnp.int32, qa_rows.shape, 1)
    qa_rows = jnp.where(r // t_new == c // DH_A, qa_rows, 0.0)
    row_t = (lax.broadcasted_iota(jnp.int32, (ra, 1), 0) % t_new).astype(F32)
    row_slope = row_const(ra, SLOPES_A, t_new)
    pages_per_block = MOBA_BLOCK // PAGE_SIZE
    n_past_blocks = PAST_LEN // MOBA_BLOCK
    means = []
    for b in range(n_past_blocks):
        tot = jnp.sum(mk[b * pages_per_block][...], axis=0, keepdims=True)
        for j in range(1, pages_per_block):
            tot = tot + jnp.sum(mk[b * pages_per_block + j][...], axis=0, keepdims=True)
        means.append(tot * (1.0 / MOBA_BLOCK))
    km = jnp.concatenate(means + [jnp.zeros((LANES - n_past_blocks, MIX_A), F32)], axis=0)
    gate = lax.dot_general(qa_rows, km, _NT, precision=lax.Precision.HIGHEST,
                           preferred_element_type=F32)
    lane_g = lax.broadcasted_iota(jnp.int32, gate.shape, 1)
    gate = jnp.where(lane_g < n_past_blocks, gate, _NEG)
    _, _, sel = _top_picks(gate, lane_g, MOBA_TOPK)
    gate_sel = [jnp.max(jnp.where(lane_g == b, sel, 0.0), axis=-1, keepdims=True) > 0.5
                for b in range(n_past_blocks)]
    ps, l = attend(qa_rows, mk, ka_ref[...], row_slope, row_t, gate_sel)
    oa = weighted_values(ps, mv, va_ref[...]) / l
    oa_ref[...] = fold_heads(oa, DH_A, H_A)

    rc = H_C * t_new
    qc = qc_ref[...] * (DH_C ** -0.5)
    qc_rows = jnp.concatenate([qc] * (2 * H_C), axis=0)
    r = lax.broadcasted_iota(jnp.int32, qc_rows.shape, 0)
    c = lax.broadcasted_iota(jnp.int32, qc_rows.shape, 1)
    keep = ((r % rc) // t_new == c // (2 * DH_C)) & (r // rc == (c % (2 * DH_C)) // DH_C)
    qc_rows = jnp.where(keep, qc_rows, 0.0)
    row_t = (lax.broadcasted_iota(jnp.int32, (2 * rc, 1), 0) % t_new).astype(F32)
    row_slope = row_const(2 * rc, SLOPES_C, t_new)
    ps, l = attend(qc_rows, dk, kc_ref[...], row_slope, row_t, None)
    lam = _lambda(lq1, lk1, lq2, lk2, lam_init)
    inv1 = 1.0 / l[0:rc]
    inv2 = lam / l[rc:2 * rc]
    ws = [p[0:rc] * inv1 - p[rc:2 * rc] * inv2 for p in ps]
    oc = fold_heads(weighted_values(ws, dv, vc_ref[...]), 2 * DH_C, H_C)
    for h in range(H_C):
        cols = slice(h * LANES, (h + 1) * LANES)
        o = oc[:, cols]
        o = o * lax.rsqrt(jnp.mean(o * o, axis=-1, keepdims=True) + LN_EPS) * ng_ref[...]
        oc_ref[:, cols] = o * (1.0 - lam_init)


def _sample_attn(layer, page_table, qa, ka, va, qc, kc, vc, pools, lam_vecs, norm_g, lam_init):
    n_seq = page_table.shape[0]
    t_new = qa.shape[0] // n_seq
    pt_flat = page_table.reshape(-1)

    def new_spec(w):
        return pl.BlockSpec((t_new, w), lambda b, pt: (b, 0))

    def page_spec(w, j):
        return pl.BlockSpec((None, None, PAGE_SIZE, w),
                            lambda b, pt, j=j: (layer, pt[b * N_PAGES + j], 0, 0))

    vec = pl.BlockSpec((1, DH_C), lambda b, pt: (0, 0))
    in_specs = [new_spec(MIX_A)] * 3 + [new_spec(MIX_C)] * 3 + [vec] * 4
    in_specs.append(pl.BlockSpec((1, 2 * DH_C), lambda b, pt: (0, 0)))
    args = [qa, ka, va, qc, kc, vc, *lam_vecs, norm_g.reshape(1, 2 * DH_C)]
    for pool in pools:
        w = pool.shape[-1]
        in_specs += [page_spec(w, j) for j in range(N_PAGES)]
        args += [pool] * N_PAGES
    grid_spec = pltpu.PrefetchScalarGridSpec(
        num_scalar_prefetch=1, grid=(n_seq,), in_specs=in_specs,
        out_specs=[new_spec(MIX_A), new_spec(MIX_C)])
    return pl.pallas_call(
        functools.partial(_sample_attn_kernel, lam_init=lam_init, t_new=t_new),
        grid_spec=grid_spec,
        out_shape=[jax.ShapeDtypeStruct((n_seq * t_new, MIX_A), F32),
                   jax.ShapeDtypeStruct((n_seq * t_new, MIX_C), F32)],
        compiler_params=_params(1),
        name="sample_attn",
    )(pt_flat, *args)


def _conv_kernel(ua_ref, ub_ref, hist_ref, cw_ref, cb_ref, g_ref, b_ref, o_ref, st_ref, z_ref,
                 *, n_seq, t_len, chunk):
    pad = hist_ref.shape[1]
    lead = pad - (CONV_W - 1)
    for s in range(n_seq):
        rows = slice(s * t_len, (s + 1) * t_len)
        z_ref[0:pad, :] = hist_ref[s]
        z_ref[pad:pad + t_len, :] = ua_ref[rows, :] * _sigmoid(ub_ref[rows, :])
        for c0 in range(0, t_len, chunk):
            acc = jnp.zeros((chunk, C_B), F32) + cb_ref[...]
            for w in range(CONV_W):
                lo = c0 + lead + w
                acc = acc + z_ref[lo:lo + chunk, :] * cw_ref[w:w + 1, :]
            y = _layernorm(acc, g_ref[...], b_ref[...])
            o_ref[s * t_len + c0:s * t_len + c0 + chunk, :] = y * _sigmoid(y)
        st_ref[s] = z_ref[lead + t_len:pad + t_len, :]


def _conv_module(proj, row0, n_seq_total, t_len, seq_per_step, hist, conv_w, conv_b, ln_g, ln_b,
                 name):
    rows = seq_per_step * t_len
    blk0 = row0 // rows
    pad = hist.shape[1]
    vec = pl.BlockSpec((1, C_B), lambda i: (0, 0))
    return pl.pallas_call(
        functools.partial(_conv_kernel, n_seq=seq_per_step, t_len=t_len, chunk=min(t_len, 256)),
        grid=(n_seq_total // seq_per_step,),
        in_specs=[pl.BlockSpec((rows, C_B), lambda i: (blk0 + i, COL_UA // C_B)),
                  pl.BlockSpec((rows, C_B), lambda i: (blk0 + i, COL_UB // C_B)),
                  pl.BlockSpec((seq_per_step, pad, C_B), lambda i: (i, 0, 0)),
                  pl.BlockSpec((CONV_W, C_B), lambda i: (0, 0)),
                  vec, vec, vec],
        out_specs=[pl.BlockSpec((rows, C_B), lambda i: (i, 0)),
                   pl.BlockSpec((seq_per_step, CONV_W - 1, C_B), lambda i: (i, 0, 0))],
        out_shape=[jax.ShapeDtypeStruct((n_seq_total * t_len, C_B), F32),
                   jax.ShapeDtypeStruct((n_seq_total, CONV_W - 1, C_B), F32)],
        scratch_shapes=[pltpu.VMEM((pad + t_len, C_B), F32)],
        compiler_params=_params(1),
        name=name,
    )(proj, proj, hist, conv_w, conv_b.reshape(1, C_B), ln_g.reshape(1, C_B), ln_b.reshape(1, C_B))


def _merge_kernel(oa_ref, ob_ref, oc_ref, g0_ref, g1_ref, g2_ref, x_ref, wa_ref, wb_ref, wc_ref,
                  wo_ref, g_ref, b_ref, o_ref):
    def branch(o_r, w_r, gate_r):
        y = jnp.dot(o_r[...].astype(BF), w_r[...], preferred_element_type=F32)
        return _sigmoid(gate_r[...]) * y

    merged = branch(oa_ref, wa_ref, g0_ref) + branch(ob_ref, wb_ref, g1_ref) + branch(oc_ref, wc_ref, g2_ref)
    y = jnp.dot(merged.astype(BF), wo_ref[...], preferred_element_type=F32)
    o_ref[...] = _layernorm(DN_ALPHA * x_ref[...] + y, g_ref[...], b_ref[...])


def _merge(oa, ob, oc, proj, x, wa, wb, wc, wo, g, b, *, tm):
    m = x.shape[0]
    d = D_MODEL

    def rows(w, col=0):
        return pl.BlockSpec((tm, w), lambda i, col=col: (i, col))

    def whole(a):
        return pl.BlockSpec(a.shape, lambda i: (0, 0))

    vec = pl.BlockSpec((1, d), lambda i: (0, 0))
    return pl.pallas_call(
        _merge_kernel,
        grid=(pl.cdiv(m, tm),),
        in_specs=[rows(MIX_A), rows(C_B), rows(MIX_C), rows(d, 0), rows(d, 1), rows(d, 2), rows(d),
                  whole(wa), whole(wb), whole(wc), whole(wo), vec, vec],
        out_specs=rows(d),
        out_shape=jax.ShapeDtypeStruct((m, d), F32),
        compiler_params=_params(1),
        name="merge",
    )(oa, ob, oc, proj, proj, proj, x, wa, wb, wc, wo, g.reshape(1, d), b.reshape(1, d))


def _xattn_kernel(q_ref, mk_ref, mv_ref, o_ref, *, n_seq, t_len):
    for s in range(n_seq):
        rows = slice(s * t_len, (s + 1) * t_len)
        for h in range(XH):
            cols = slice(h * XDH, (h + 1) * XDH)
            qh = (q_ref[rows, cols] * (XDH ** -0.5)).astype(BF)
            sc = lax.dot_general(qh, mk_ref[s, :, cols].astype(BF), _NT, preferred_element_type=F32)
            m = jnp.max(sc, axis=-1, keepdims=True)
            p = jnp.exp(sc - m)
            l = jnp.sum(p, axis=-1, keepdims=True)
            o = jnp.dot(p.astype(BF), mv_ref[s, :, cols].astype(BF), preferred_element_type=F32)
            o_ref[rows, cols] = o / l


def _xattn(q, row0, n_rows, mem_k, mem_v, layer, seq_per_step, t_len, steps_per_mem, name):
    rows = seq_per_step * t_len
    blk0 = row0 // rows
    if mem_k.ndim == 4:
        mem_spec = pl.BlockSpec((None, seq_per_step, MEM_LEN, D_MODEL),
                                lambda i: (layer, i // steps_per_mem, 0, 0))
    else:
        mem_spec = pl.BlockSpec((seq_per_step, MEM_LEN, D_MODEL), lambda i: (i // steps_per_mem, 0, 0))
    return pl.pallas_call(
        functools.partial(_xattn_kernel, n_seq=seq_per_step, t_len=t_len),
        grid=(n_rows // rows,),
        in_specs=[pl.BlockSpec((rows, D_MODEL), lambda i: (blk0 + i, 0)), mem_spec, mem_spec],
        out_specs=pl.BlockSpec((rows, D_MODEL), lambda i: (i, 0)),
        out_shape=jax.ShapeDtypeStruct((n_rows, D_MODEL), F32),
        compiler_params=_params(1),
        name=name,
    )(q, mem_k, mem_v)


def _router_kernel(x_ref, w_ref, b_ref, idx_ref, gate_ref):
    logits = jnp.dot(x_ref[...], w_ref[...], precision=lax.Precision.HIGHEST,
                     preferred_element_type=F32) + b_ref[...]
    lane = lax.broadcasted_iota(jnp.int32, logits.shape, 1)
    logits = jnp.where(lane < N_EXPERTS, logits, _NEG)
    vals, idxs, _ = _top_picks(logits, lane, TOP_K)
    es = [jnp.exp(v - vals[0]) for v in vals]
    denom = es[0]
    for e in es[1:]:
        denom = denom + e
    idx_out = jnp.zeros(logits.shape, jnp.int32)
    gate_out = jnp.zeros(logits.shape, F32)
    for k in range(TOP_K):
        idx_out = jnp.where(lane == k, idxs[k].astype(jnp.int32), idx_out)
        gate_out = jnp.where(lane == k, es[k] / denom, gate_out)
    idx_ref[...] = idx_out
    gate_ref[...] = gate_out


def _router(x, w_pad, b_pad, *, tm):
    m, d = x.shape
    return pl.pallas_call(
        _router_kernel,
        grid=(pl.cdiv(m, tm),),
        in_specs=[pl.BlockSpec((tm, d), lambda i: (i, 0)),
                  pl.BlockSpec((d, LANES), lambda i: (0, 0)),
                  pl.BlockSpec((1, LANES), lambda i: (0, 0))],
        out_specs=[pl.BlockSpec((tm, LANES), lambda i: (i, 0))] * 2,
        out_shape=[jax.ShapeDtypeStruct((m, LANES), jnp.int32),
                   jax.ShapeDtypeStruct((m, LANES), F32)],
        compiler_params=_params(1),
        name="router",
    )(x, w_pad, b_pad)


def _experts_kernel(be_ref, nv_ref, x_ref, wgu_ref, bgu_ref, wd_ref, bd_ref, o_ref, wgu_b, wd_b):
    i = pl.program_id(0)
    e = be_ref[i]
    prev = be_ref[jnp.maximum(i - 1, 0)]
    live = i < nv_ref[0]

    @pl.when(live & ((i == 0) | (e != prev)))
    def _():
        wgu_b[...] = wgu_ref[...].astype(BF)
        wd_b[...] = wd_ref[...].astype(BF)

    @pl.when(live)
    def _():
        h = jnp.dot(x_ref[...].astype(BF), wgu_b[...], preferred_element_type=F32) + bgu_ref[...]
        hg = jnp.minimum(h[:, :D_FF], SWIGLU_LIMIT)
        hl = jnp.clip(h[:, D_FF:], -SWIGLU_LIMIT, SWIGLU_LIMIT)
        a = hg * _sigmoid(SWIGLU_ALPHA * hg) * (hl + 1.0)
        o_ref[...] = jnp.dot(a.astype(BF), wd_b[...], preferred_element_type=F32) + bd_ref[...]

    @pl.when(jnp.logical_not(live))
    def _():
        o_ref[...] = jnp.zeros(o_ref.shape, F32)


def _experts(layer, blk_e, n_live, xb, w_gate_up, b_gate_up, w_down, b_down):
    cap, d = xb.shape
    n_blocks = cap // MOE_BLOCK
    grid_spec = pltpu.PrefetchScalarGridSpec(
        num_scalar_prefetch=2, grid=(n_blocks,),
        in_specs=[pl.BlockSpec((MOE_BLOCK, d), lambda i, be, nv: (i, 0)),
                  pl.BlockSpec((None, None, d, 2 * D_FF), lambda i, be, nv: (layer, be[i], 0, 0)),
                  pl.BlockSpec((None, None, 1, 2 * D_FF), lambda i, be, nv: (layer, be[i], 0, 0)),
                  pl.BlockSpec((None, None, D_FF, d), lambda i, be, nv: (layer, be[i], 0, 0)),
                  pl.BlockSpec((None, None, 1, d), lambda i, be, nv: (layer, be[i], 0, 0))],
        out_specs=pl.BlockSpec((MOE_BLOCK, d), lambda i, be, nv: (i, 0)),
        scratch_shapes=[pltpu.VMEM((d, 2 * D_FF), BF), pltpu.VMEM((D_FF, d), BF)])
    return pl.pallas_call(
        _experts_kernel,
        grid_spec=grid_spec,
        out_shape=jax.ShapeDtypeStruct((cap, d), F32),
        compiler_params=_params(1),
        name="experts",
    )(blk_e, n_live, xb, w_gate_up, b_gate_up.reshape(DEPTH, N_EXPERTS, 1, 2 * D_FF),
      w_down, b_down.reshape(DEPTH, N_EXPERTS, 1, d))


def _moe(layer, h, router_w_pad, router_b_pad, w_gate_up, b_gate_up, w_down, b_down):
    n, d = h.shape
    idx, gate = _router(h, router_w_pad, router_b_pad, tm=512)
    top_i = idx[:, :TOP_K]
    top_g = gate[:, :TOP_K]
    n_flat = n * TOP_K
    key_bits = (n_flat - 1).bit_length()
    flat_e = top_i.reshape(-1)
    flat_i = jnp.arange(n_flat, dtype=jnp.int32)
    sorted_keys = jnp.sort(flat_e * (1 << key_bits) + flat_i)
    st = (sorted_keys & ((1 << key_bits) - 1)) // TOP_K
    onehot = (flat_e[:, None] == jnp.arange(N_EXPERTS, dtype=jnp.int32)[None, :]).astype(jnp.int32)
    ranks = jnp.cumsum(onehot, axis=0)
    counts = ranks[-1]
    padded = (counts + MOE_BLOCK - 1) // MOE_BLOCK * MOE_BLOCK
    pad_end = jnp.cumsum(padded)
    pad_start = pad_end - padded
    start = jnp.cumsum(counts) - counts
    n_blocks = n_flat // MOE_BLOCK + N_EXPERTS
    cap = n_blocks * MOE_BLOCK
    blk_e = jnp.minimum(jnp.searchsorted(pad_end, jnp.arange(n_blocks, dtype=jnp.int32) * MOE_BLOCK,
                                         side='right'), N_EXPERTS - 1).astype(jnp.int32)
    n_live = (pad_end[-1:] // MOE_BLOCK).astype(jnp.int32)
    row = jnp.arange(cap, dtype=jnp.int32)
    row_e = blk_e[row // MOE_BLOCK]
    row_off = row - pad_start[row_e]
    row_ok = (row_off < counts[row_e]) & (row < pad_end[-1])
    buf_t = jnp.where(row_ok, st[jnp.clip(start[row_e] + row_off, 0, n_flat - 1)], 0)
    pos = pad_start[flat_e] + jnp.sum(ranks * onehot, axis=1) - 1
    xb = h[buf_t]
    yb = _experts(layer, blk_e, n_live, xb, w_gate_up, b_gate_up, w_down, b_down)
    return jnp.sum(yb[pos].reshape(n, TOP_K, d) * top_g[:, :, None], axis=1)


def kernel(x_prompt, x_sample, mem_prompt, cache_moba_k, cache_moba_v, cache_diff_k, cache_diff_v, cache_mem_k, cache_mem_v, state_conv, page_table, w_in, b_in, conv_w, conv_b, conv_ln_g, conv_ln_b, lam_q1, lam_k1, lam_q2, lam_k2, diff_norm_g, w_branch, w_out, ln1_g, ln1_b, xq_w, xk_w, xv_w, xo_w, ln2_g, ln2_b, router_w, router_b, w_gate_up, b_gate_up, w_down, b_down, ln3_g, ln3_b):
    bp, seq, d = x_prompt.shape
    db, dseq, _ = x_sample.shape
    n_p = bp * seq
    n_s = db * dseq
    n_pool = cache_moba_k.shape[1]
    hist_pad = 32

    o_qa, o_ka, o_va = 0, MIX_A, 2 * MIX_A
    o_u = 3 * MIX_A
    o_qc = o_u + 2 * C_B
    o_kc, o_vc = o_qc + MIX_C, o_qc + 2 * MIX_C
    o_g = o_qc + 3 * MIX_C

    def permute_cols(a):
        return jnp.concatenate([a[..., o_g:], a[..., o_qc:o_g], a[..., o_qa:o_u], a[..., o_u:o_qc]], axis=-1)

    pools = [c.reshape(DEPTH, n_pool, PAGE_SIZE, -1)
             for c in (cache_moba_k, cache_moba_v, cache_diff_k, cache_diff_v)]
    mem_k_s = cache_mem_k.reshape(DEPTH, db, MEM_LEN, d)
    mem_v_s = cache_mem_v.reshape(DEPTH, db, MEM_LEN, d)
    mem2d = mem_prompt.reshape(bp * MEM_LEN, d)
    hist_p = jnp.zeros((bp, hist_pad, C_B), F32)
    hist_s = jnp.pad(state_conv, ((0, 0), (0, 0), (hist_pad - (CONV_W - 1), 0), (0, 0)))
    router_w_pad = jnp.pad(router_w, ((0, 0), (0, 0), (0, LANES - N_EXPERTS)))
    router_b_pad = jnp.pad(router_b, ((0, 0), (0, LANES - N_EXPERTS))).reshape(DEPTH, 1, LANES)

    x = jnp.concatenate([x_prompt.reshape(n_p, d), x_sample.reshape(n_s, d)], axis=0)
    outs = {k: [] for k in ("ka_p", "va_p", "kc_p", "vc_p", "mk_p", "mv_p", "cs_p",
                            "ka_s", "va_s", "kc_s", "vc_s", "cs_s")}

    for l in range(DEPTH):
        lam_init = 0.8 - 0.6 * math.exp(-0.3 * l)
        lam_vecs = [v[l].reshape(1, DH_C) for v in (lam_q1, lam_k1, lam_q2, lam_k2)]
        w_in_l = permute_cols(w_in[l]).astype(BF)
        b_in_l = permute_cols(b_in[l])
        wa = w_branch[l, :MIX_A].astype(BF)
        wb = w_branch[l, MIX_A:MIX_A + C_B].astype(BF)
        wc = w_branch[l, MIX_A + C_B:].astype(BF)

        proj = _matmul(x, w_in_l, b_in_l, tm=512, tn=IN_W // 2, name="proj_in")
        proj_s = proj[n_p:]
        new = {k: proj_s[:, c:c + w] for k, c, w in
               (("qa", COL_QA, MIX_A), ("ka", COL_KA, MIX_A), ("va", COL_VA, MIX_A),
                ("qc", COL_QC, MIX_C), ("kc", COL_KC, MIX_C), ("vc", COL_VC, MIX_C))}

        oa_p = _moba_prompt(proj, bp, seq)
        oc_p = _diff_prompt(proj, lam_vecs, diff_norm_g[l], bp, seq, lam_init)
        oa_s, oc_s = _sample_attn(l, page_table, new["qa"], new["ka"], new["va"],
                                  new["qc"], new["kc"], new["vc"], pools, lam_vecs,
                                  diff_norm_g[l], lam_init)
        ob_p, cs_p = _conv_module(proj, 0, bp, seq, 1, hist_p, conv_w[l], conv_b[l],
                                  conv_ln_g[l], conv_ln_b[l], "conv_prompt")
        ob_s, cs_s = _conv_module(proj, n_p, db, dseq, 16, hist_s[l], conv_w[l], conv_b[l],
                                  conv_ln_g[l], conv_ln_b[l], "conv_sample")
        oa = jnp.concatenate([oa_p, oa_s], axis=0)
        ob = jnp.concatenate([ob_p, ob_s], axis=0)
        oc = jnp.concatenate([oc_p, oc_s], axis=0)
        h1 = _merge(oa, ob, oc, proj, x, wa, wb, wc, w_out[l].astype(BF), ln1_g[l], ln1_b[l], tm=256)

        mk_p = _matmul(mem2d, xk_w[l].astype(BF), tm=512, tn=d, name="mem_k")
        mv_p = _matmul(mem2d, xv_w[l].astype(BF), tm=512, tn=d, name="mem_v")
        q = _matmul(h1, xq_w[l].astype(BF), tm=1024, tn=d, name="xattn_q")
        xo_p = _xattn(q, 0, n_p, mk_p.reshape(bp, MEM_LEN, d), mv_p.reshape(bp, MEM_LEN, d),
                      l, 1, 512, seq // 512, "xattn_prompt")
        xo_s = _xattn(q, n_p, n_s, mem_k_s, mem_v_s, l, 4, dseq, 1, "xattn_sample")
        xo = jnp.concatenate([xo_p, xo_s], axis=0)
        h2 = _mm_res_ln(xo, xo_w[l].astype(BF), h1, ln2_g[l], ln2_b[l], tm=512, name="xattn_out")

        y = _moe(l, h2, router_w_pad[l], router_b_pad[l], w_gate_up, b_gate_up, w_down, b_down)
        x = _res_ln(y, h2, ln3_g[l], ln3_b[l], tm=512, name="moe_out")

        proj_p = proj[:n_p]
        outs["ka_p"].append(proj_p[:, COL_KA:COL_KA + MIX_A].reshape(bp, seq, H_A, DH_A))
        outs["va_p"].append(proj_p[:, COL_VA:COL_VA + MIX_A].reshape(bp, seq, H_A, DH_A))
        outs["kc_p"].append(proj_p[:, COL_KC:COL_KC + MIX_C].reshape(bp, seq, H_C, 2 * DH_C))
        outs["vc_p"].append(proj_p[:, COL_VC:COL_VC + MIX_C].reshape(bp, seq, H_C, 2 * DH_C))
        outs["mk_p"].append(mk_p.reshape(bp, MEM_LEN, XH, XDH))
        outs["mv_p"].append(mv_p.reshape(bp, MEM_LEN, XH, XDH))
        outs["cs_p"].append(cs_p)
        outs["ka_s"].append(new["ka"].reshape(db, dseq, H_A, DH_A))
        outs["va_s"].append(new["va"].reshape(db, dseq, H_A, DH_A))
        outs["kc_s"].append(new["kc"].reshape(db, dseq, H_C, 2 * DH_C))
        outs["vc_s"].append(new["vc"].reshape(db, dseq, H_C, 2 * DH_C))
        outs["cs_s"].append(cs_s)

    return (x[:n_p].reshape(bp, seq, d), x[n_p:].reshape(db, dseq, d),
            jnp.stack(outs["ka_p"]), jnp.stack(outs["va_p"]), jnp.stack(outs["kc_p"]),
            jnp.stack(outs["vc_p"]), jnp.stack(outs["mk_p"]), jnp.stack(outs["mv_p"]),
            jnp.stack(outs["cs_p"]), jnp.stack(outs["ka_s"]), jnp.stack(outs["va_s"]),
            jnp.stack(outs["kc_s"]), jnp.stack(outs["vc_s"]), jnp.stack(outs["cs_s"]))
```

```python
import functools
import math

import jax
import jax.numpy as jnp
from jax import lax
from jax.experimental import pallas as pl
from jax.experimental.pallas import tpu as pltpu

D_MODEL = 1024
DEPTH = 2
PAST_LEN = 2048
PAGE_SIZE = 128
N_PAGES = PAST_LEN // PAGE_SIZE

H_A = 4
DH_A = 64
MOBA_BLOCK = 256
MOBA_TOPK = 3
C_B = 256
CONV_W = 31
H_C = 4
DH_C = 64
MEM_LEN = 256
XH = 4
XDH = D_MODEL // XH
N_EXPERTS = 32
TOP_K = 4
D_FF = D_MODEL
SWIGLU_LIMIT = 7.0
SWIGLU_ALPHA = 1.702
MIX_A = H_A * DH_A
MIX_C = H_C * 2 * DH_C
N_BRANCH = 3
DN_ALPHA = (2 * DEPTH) ** 0.25
LN_EPS = 1e-5
LOG2E = math.log2(math.e)

SLOPES_C = tuple(2.0 ** -(i + 1) for i in range(0, 4))
SLOPES_A = tuple(2.0 ** -(i + 1) for i in range(4, 8))

COL_G = 0
COL_QC = 3 * D_MODEL
COL_KC = COL_QC + MIX_C
COL_VC = COL_KC + MIX_C
COL_QA = COL_VC + MIX_C
COL_KA = COL_QA + MIX_A
COL_VA = COL_KA + MIX_A
COL_UA = COL_VA + MIX_A
COL_UB = COL_UA + C_B
IN_W = COL_UB + C_B

LANES = 128
SUBLANES = 8
Q_TILE = MOBA_BLOCK
KEY_TILE = MOBA_BLOCK
MOE_BLOCK = 256
VMEM_LIMIT = 56 * 1024 * 1024

_NT = (((1,), (1,)), ((), ()))
_NEG = -jnp.inf
BF = jnp.bfloat16
F32 = jnp.float32


def _params(n_grid):
    return pltpu.CompilerParams(dimension_semantics=("arbitrary",) * n_grid,
                                vmem_limit_bytes=VMEM_LIMIT)


def _sigmoid(x):
    return 1.0 / (1.0 + jnp.exp(-x))


def _layernorm(x, g, b):
    mu = jnp.mean(x, axis=-1, keepdims=True)
    xc = x - mu
    var = jnp.mean(xc * xc, axis=-1, keepdims=True)
    return xc * lax.rsqrt(var + LN_EPS) * g + b


def _top_picks(g, pos, k, axis=-1):
    lane = pos.astype(F32)
    sel = jnp.zeros(g.shape, F32)
    vals, idxs = [], []
    for _ in range(k):
        m = jnp.max(g, axis=axis, keepdims=True)
        cand = jnp.where(g == m, lane, float(g.shape[axis]))
        idx = jnp.where(m > _NEG, jnp.min(cand, axis=axis, keepdims=True), -1.0)
        pick = lane == idx
        sel = jnp.where(pick, 1.0, sel)
        g = jnp.where(pick, _NEG, g)
        vals.append(m)
        idxs.append(idx)
    return vals, idxs, sel


def _mm_kernel(x_ref, w_ref, *rest, has_bias):
    o_ref = rest[-1]
    acc = jnp.dot(x_ref[...].astype(BF), w_ref[...], preferred_element_type=F32)
    if has_bias:
        acc = acc + rest[0][...]
    o_ref[...] = acc


def _matmul(x, w, b=None, *, tm, tn, name):
    m, k = x.shape
    n = w.shape[1]
    in_specs = [pl.BlockSpec((tm, k), lambda i, j: (i, 0)),
                pl.BlockSpec((k, tn), lambda i, j: (0, j))]
    args = [x, w]
    if b is not None:
        in_specs.append(pl.BlockSpec((1, tn), lambda i, j: (0, j)))
        args.append(b.reshape(1, n))
    return pl.pallas_call(
        functools.partial(_mm_kernel, has_bias=b is not None),
        grid=(pl.cdiv(m, tm), n // tn),
        in_specs=in_specs,
        out_specs=pl.BlockSpec((tm, tn), lambda i, j: (i, j)),
        out_shape=jax.ShapeDtypeStruct((m, n), F32),
        compiler_params=_params(2),
        name=name,
    )(*args)


def _mm_res_ln_kernel(x_ref, w_ref, r_ref, g_ref, b_ref, o_ref):
    y = jnp.dot(x_ref[...].astype(BF), w_ref[...], preferred_element_type=F32)
    o_ref[...] = _layernorm(DN_ALPHA * r_ref[...] + y, g_ref[...], b_ref[...])


def _mm_res_ln(x, w, res, g, b, *, tm, name):
    m, k = x.shape
    n = w.shape[1]
    return pl.pallas_call(
        _mm_res_ln_kernel,
        grid=(pl.cdiv(m, tm),),
        in_specs=[pl.BlockSpec((tm, k), lambda i: (i, 0)),
                  pl.BlockSpec((k, n), lambda i: (0, 0)),
                  pl.BlockSpec((tm, n), lambda i: (i, 0)),
                  pl.BlockSpec((1, n), lambda i: (0, 0)),
                  pl.BlockSpec((1, n), lambda i: (0, 0))],
        out_specs=pl.BlockSpec((tm, n), lambda i: (i, 0)),
        out_shape=jax.ShapeDtypeStruct((m, n), F32),
        compiler_params=_params(1),
        name=name,
    )(x, w, res, g.reshape(1, n), b.reshape(1, n))


def _res_ln_kernel(y_ref, r_ref, g_ref, b_ref, o_ref):
    o_ref[...] = _layernorm(DN_ALPHA * r_ref[...] + y_ref[...], g_ref[...], b_ref[...])


def _res_ln(y, res, g, b, *, tm, name):
    m, n = y.shape
    return pl.pallas_call(
        _res_ln_kernel,
        grid=(pl.cdiv(m, tm),),
        in_specs=[pl.BlockSpec((tm, n), lambda i: (i, 0)),
                  pl.BlockSpec((tm, n), lambda i: (i, 0)),
                  pl.BlockSpec((1, n), lambda i: (0, 0)),
                  pl.BlockSpec((1, n), lambda i: (0, 0))],
        out_specs=pl.BlockSpec((tm, n), lambda i: (i, 0)),
        out_shape=jax.ShapeDtypeStruct((m, n), F32),
        compiler_params=_params(1),
        name=name,
    )(y, res, g.reshape(1, n), b.reshape(1, n))


def _chain_start(s, v_t):
    m = jnp.max(s, axis=0, keepdims=True)
    p = jnp.exp2(s - m)
    l = jnp.sum(p, axis=0, keepdims=True)
    return m, l, jnp.dot(v_t, p.astype(BF), preferred_element_type=F32)


def _chain_step(state, s, shift, v_t):
    m, l, acc = state
    m_new = jnp.maximum(m, jnp.max(s, axis=0, keepdims=True) - shift)
    a = jnp.exp2(m - m_new)
    p = jnp.exp2(s - (m_new + shift))
    l = a * l + jnp.sum(p, axis=0, keepdims=True)
    return m_new, l, a * acc + jnp.dot(v_t, p.astype(BF), preferred_element_type=F32)


def _tile_geometry():
    key_j = lax.broadcasted_iota(jnp.int32, (KEY_TILE, Q_TILE), 0)
    qry_i = lax.broadcasted_iota(jnp.int32, (KEY_TILE, Q_TILE), 1)
    causal = qry_i >= key_j
    rel = (key_j - (KEY_TILE - 1)).astype(F32)
    return causal, rel


def _stage_kv(k_ref, v_ref, kb_ref, vt_ref, n_blocks):
    kb_ref[...] = k_ref[...].astype(BF)
    for n in range(n_blocks):
        vt_ref[n] = v_ref[n * KEY_TILE:(n + 1) * KEY_TILE, :].T.astype(BF)


def _moba_prompt_kernel(q_ref, k_ref, v_ref, o_ref, kb_ref, vt_ref, km_ref, *, n_blocks):
    cur = pl.program_id(1)

    @pl.when(cur == 0)
    def _():
        _stage_kv(k_ref, v_ref, kb_ref, vt_ref, n_blocks)
        km_ref[...] = jnp.zeros(km_ref.shape, F32)
        for n in range(n_blocks):
            km_ref[n:n + 1, :] = jnp.mean(k_ref[n * MOBA_BLOCK:(n + 1) * MOBA_BLOCK, :],
                                          axis=0, keepdims=True)

    causal, rel = _tile_geometry()
    q = q_ref[...]
    lane_q = lax.broadcasted_iota(jnp.int32, q.shape, 1)
    lane_p = lax.broadcasted_iota(jnp.int32, (Q_TILE, LANES), 1)
    blk_row = lax.broadcasted_iota(jnp.int32, (km_ref.shape[0], Q_TILE), 0)
    own = pl.multiple_of(cur * KEY_TILE, KEY_TILE)

    consts, init = [], []
    for h in range(H_A):
        pair, hh = divmod(h, 2)
        cols = slice(pair * LANES, (pair + 1) * LANES)
        slope2 = SLOPES_A[h] * LOG2E
        gate = lax.dot_general(km_ref[...], jnp.where(lane_q // DH_A == h, q, 0.0), _NT,
                               precision=lax.Precision.HIGHEST,
                               preferred_element_type=F32)
        gate = jnp.where(blk_row < cur, gate, _NEG)
        _, _, sel = _top_picks(gate, blk_row, MOBA_TOPK, axis=0)
        qm = jnp.where(lane_p // DH_A == hh, q[:, cols] * (DH_A ** -0.5 * LOG2E), 0.0).astype(BF)
        bias = slope2 * rel
        s = lax.dot_general(kb_ref[pl.ds(own, KEY_TILE), cols], qm, _NT,
                            preferred_element_type=F32) + bias
        init.append(_chain_start(jnp.where(causal, s, _NEG), vt_ref[cur, cols, :]))
        consts.append((cols, slope2, sel, qm, bias))

    def body(n, carry):
        start = pl.multiple_of(n * KEY_TILE, KEY_TILE)
        dist = jnp.full((1, Q_TILE), (cur - n) * KEY_TILE, jnp.int32).astype(F32)
        out = []
        for (cols, slope2, sel, qm, bias), state in zip(consts, carry):
            picked = jnp.max(jnp.where(blk_row == n, sel, 0.0), axis=0, keepdims=True) > 0.5
            s = lax.dot_general(kb_ref[pl.ds(start, KEY_TILE), cols], qm, _NT,
                                preferred_element_type=F32) + bias
            out.append(_chain_step(state, jnp.where(picked, s, _NEG), slope2 * dist,
                                   vt_ref[n, cols, :]))
        return tuple(out)

    final = lax.fori_loop(0, cur, body, tuple(init))
    row = lax.broadcasted_iota(jnp.int32, (LANES, Q_TILE), 0)
    for pair in range(H_A // 2):
        (_, l0, a0), (_, l1, a1) = final[2 * pair], final[2 * pair + 1]
        o_t = jnp.where(row < DH_A, a0 / l0, a1 / l1)
        o_ref[:, pair * LANES:(pair + 1) * LANES] = o_t.T


def _moba_prompt(proj, batch, seq):
    n_q = seq // Q_TILE
    n_blocks = seq // MOBA_BLOCK
    km_rows = -(-n_blocks // SUBLANES) * SUBLANES
    cq, ck, cv = COL_QA // MIX_A, COL_KA // MIX_A, COL_VA // MIX_A
    return pl.pallas_call(
        functools.partial(_moba_prompt_kernel, n_blocks=n_blocks),
        grid=(batch, n_q),
        in_specs=[pl.BlockSpec((Q_TILE, MIX_A), lambda b, i: (b * n_q + i, cq)),
                  pl.BlockSpec((seq, MIX_A), lambda b, i: (b, ck)),
                  pl.BlockSpec((seq, MIX_A), lambda b, i: (b, cv))],
        out_specs=pl.BlockSpec((Q_TILE, MIX_A), lambda b, i: (b * n_q + i, 0)),
        out_shape=jax.ShapeDtypeStruct((batch * seq, MIX_A), F32),
        scratch_shapes=[pltpu.VMEM((seq, MIX_A), BF),
                        pltpu.VMEM((n_blocks, MIX_A, KEY_TILE), BF),
                        pltpu.VMEM((km_rows, MIX_A), F32)],
        compiler_params=_params(2),
        name="moba_prompt",
    )(proj, proj, proj)


def _lambda(lq1, lk1, lq2, lk2, lam_init):
    return (jnp.exp(jnp.sum(lq1[...] * lk1[...], axis=-1, keepdims=True))
            - jnp.exp(jnp.sum(lq2[...] * lk2[...], axis=-1, keepdims=True)) + lam_init)


def _diff_prompt_kernel(q_ref, k_ref, v_ref, lq1, lk1, lq2, lk2, ng_ref, o_ref, kb_ref, vt_ref,
                        *, lam_init, n_blocks):
    cur = pl.program_id(1)

    @pl.when(cur == 0)
    def _():
        _stage_kv(k_ref, v_ref, kb_ref, vt_ref, n_blocks)

    causal, rel = _tile_geometry()
    lam = _lambda(lq1, lk1, lq2, lk2, lam_init)
    lane = lax.broadcasted_iota(jnp.int32, (Q_TILE, LANES), 1)
    own = pl.multiple_of(cur * KEY_TILE, KEY_TILE)

    consts, init = [], []
    for h in range(H_C):
        cols = slice(h * LANES, (h + 1) * LANES)
        slope2 = SLOPES_C[h] * LOG2E
        qh = q_ref[:, cols] * (DH_C ** -0.5 * LOG2E)
        q1 = jnp.where(lane < DH_C, qh, 0.0).astype(BF)
        q2 = jnp.where(lane >= DH_C, qh, 0.0).astype(BF)
        bias = slope2 * rel
        kb = kb_ref[pl.ds(own, KEY_TILE), cols]
        v_t = vt_ref[cur, cols, :]
        for qm in (q1, q2):
            s = lax.dot_general(kb, qm, _NT, preferred_element_type=F32) + bias
            init.append(_chain_start(jnp.where(causal, s, _NEG), v_t))
        consts.append((cols, slope2, q1, q2, bias))

    def body(n, carry):
        start = pl.multiple_of(n * KEY_TILE, KEY_TILE)
        dist = jnp.full((1, Q_TILE), (cur - n) * KEY_TILE, jnp.int32).astype(F32)
        out = []
        for h, (cols, slope2, q1, q2, bias) in enumerate(consts):
            kb = kb_ref[pl.ds(start, KEY_TILE), cols]
            v_t = vt_ref[n, cols, :]
            for c, qm in enumerate((q1, q2)):
                s = lax.dot_general(kb, qm, _NT, preferred_element_type=F32) + bias
                out.append(_chain_step(carry[2 * h + c], s, slope2 * dist, v_t))
        return tuple(out)

    final = lax.fori_loop(0, cur, body, tuple(init))
    for h in range(H_C):
        (_, l1, a1), (_, l2, a2) = final[2 * h], final[2 * h + 1]
        o_t = a1 / l1 - lam * (a2 / l2)
        o_t = o_t * lax.rsqrt(jnp.mean(o_t * o_t, axis=0, keepdims=True) + LN_EPS) * ng_ref[...]
        o_ref[:, h * LANES:(h + 1) * LANES] = (o_t * (1.0 - lam_init)).T


def _diff_prompt(proj, lam_vecs, norm_g, batch, seq, lam_init):
    n_q = seq // Q_TILE
    n_blocks = seq // KEY_TILE
    cq, ck, cv = COL_QC // MIX_C, COL_KC // MIX_C, COL_VC // MIX_C
    vec = pl.BlockSpec((1, DH_C), lambda b, i: (0, 0))
    return pl.pallas_call(
        functools.partial(_diff_prompt_kernel, lam_init=lam_init, n_blocks=n_blocks),
        grid=(batch, n_q),
        in_specs=[pl.BlockSpec((Q_TILE, MIX_C), lambda b, i: (b * n_q + i, cq)),
                  pl.BlockSpec((seq, MIX_C), lambda b, i: (b, ck)),
                  pl.BlockSpec((seq, MIX_C), lambda b, i: (b, cv)),
                  vec, vec, vec, vec,
                  pl.BlockSpec((2 * DH_C, 1), lambda b, i: (0, 0))],
        out_specs=pl.BlockSpec((Q_TILE, MIX_C), lambda b, i: (b * n_q + i, 0)),
        out_shape=jax.ShapeDtypeStruct((batch * seq, MIX_C), F32),
        scratch_shapes=[pltpu.VMEM((seq, MIX_C), BF),
                        pltpu.VMEM((n_blocks, MIX_C, KEY_TILE), BF)],
        compiler_params=_params(2),
        name="diff_prompt",
    )(proj, proj, proj, *lam_vecs, norm_g.reshape(2 * DH_C, 1))


def _sample_attn_kernel(pt_ref, qa_ref, ka_ref, va_ref, qc_ref, kc_ref, vc_ref,
                        lq1, lk1, lq2, lk2, ng_ref, *rest, lam_init, t_new):
    del pt_ref
    n = N_PAGES
    mk, mv = rest[0:n], rest[n:2 * n]
    dk, dv = rest[2 * n:3 * n], rest[3 * n:4 * n]
    oa_ref, oc_ref = rest[4 * n], rest[4 * n + 1]
    pad_rows = PAGE_SIZE - t_new

    def pad_new(x):
        return jnp.concatenate([x, jnp.zeros((pad_rows, x.shape[1]), x.dtype)], axis=0)

    def head_rows(ref):
        return jnp.concatenate([ref[pl.ds(h, PAGE_SIZE, stride=H_C), :] for h in range(H_C)],
                               axis=1)

    def attend(q_rows, page_scores, k_new, row_slope, row_t, gate_sel):
        r = q_rows.shape[0]
        qb = q_rows.astype(BF)
        lane_f = lax.broadcasted_iota(jnp.int32, (r, PAGE_SIZE), 1).astype(F32)
        tiles = []
        for j in range(n):
            s = page_scores(qb, j)
            s = s - row_slope * (float(PAST_LEN - j * PAGE_SIZE) + row_t - lane_f)
            if gate_sel is not None:
                s = jnp.where(gate_sel[j // (MOBA_BLOCK // PAGE_SIZE)], s, _NEG)
            tiles.append(s)
        s = lax.dot_general(qb, pad_new(k_new).astype(BF), _NT, preferred_element_type=F32)
        tiles.append(jnp.where(lane_f <= row_t, s - row_slope * (row_t - lane_f), _NEG))
        m = tiles[0].max(axis=-1, keepdims=True)
        for s in tiles[1:]:
            m = jnp.maximum(m, s.max(axis=-1, keepdims=True))
        ps = [jnp.exp(s - m) for s in tiles]
        l = ps[0].sum(axis=-1, keepdims=True)
        for p in ps[1:]:
            l = l + p.sum(axis=-1, keepdims=True)
        return ps, l

    def fold_heads(o, head_w, n_heads):
        r = lax.broadcasted_iota(jnp.int32, o.shape, 0)
        c = lax.broadcasted_iota(jnp.int32, o.shape, 1)
        o = jnp.where(r // t_new == c // head_w, o, 0.0)
        out = o[0:t_new]
        for h in range(1, n_heads):
            out = out + o[h * t_new:(h + 1) * t_new]
        return out

    def row_const(rows, values):
        h = lax.broadcasted_iota(jnp.int32, (rows, 1), 0) // t_new
        out = jnp.full((rows, 1), values[-1], F32)
        for i in range(len(values) - 2, -1, -1):
            out = jnp.where(h % len(values) == i, values[i], out)
        return out

    ra = H_A * t_new
    qa = qa_ref[...] * (DH_A ** -0.5)
    qa_rows = jnp.concatenate([qa] * H_A, axis=0)
    r = lax.broadcasted_iota(jnp.int32, qa_rows.shape, 0)
    c = lax.broadcasted_iota(jnp.int32, qa_rows.shape, 1)
    qa_rows = jnp.where(r // t_new == c // DH_A, qa_rows, 0.0)
    row_t = (lax.broadcasted_iota(jnp.int32, (ra, 1), 0) % t_new).astype(F32)
    row_slope = row_const(ra, SLOPES_A)
    pages_per_block = MOBA_BLOCK // PAGE_SIZE
    n_past_blocks = PAST_LEN // MOBA_BLOCK
    lane_m = lax.broadcasted_iota(jnp.int32, (MIX_A, LANES), 1)
    km_t = jnp.zeros((MIX_A, LANES), F32)
    for b in range(n_past_blocks):
        tot = jnp.sum(mk[b * pages_per_block][...], axis=1, keepdims=True)
        for j in range(1, pages_per_block):
            tot = tot + jnp.sum(mk[b * pages_per_block + j][...], axis=1, keepdims=True)
        km_t = jnp.where(lane_m == b, tot * (1.0 / MOBA_BLOCK), km_t)
    gate = jnp.dot(qa_rows, km_t, precision=lax.Precision.HIGHEST,
                   preferred_element_type=F32)
    lane_g = lax.broadcasted_iota(jnp.int32, gate.shape, 1)
    gate = jnp.where(lane_g < n_past_blocks, gate, _NEG)
    _, _, sel = _top_picks(gate, lane_g, MOBA_TOPK)
    gate_sel = [jnp.max(jnp.where(lane_g == b, sel, 0.0), axis=-1, keepdims=True) > 0.5
                for b in range(n_past_blocks)]
    ps, l = attend(qa_rows,
                   lambda qb, j: jnp.dot(qb, mk[j][...].astype(BF), preferred_element_type=F32),
                   ka_ref[...], row_slope, row_t, gate_sel)
    oa = jnp.dot(ps[n].astype(BF), pad_new(va_ref[...]).astype(BF), preferred_element_type=F32)
    for j in range(n):
        oa = oa + lax.dot_general(ps[j].astype(BF), mv[j][...].astype(BF), _NT,
                                  preferred_element_type=F32)
    oa_ref[...] = fold_heads(oa / l, DH_A, H_A)

    rc = H_C * t_new
    qc = qc_ref[...] * (DH_C ** -0.5)
    qc_rows = jnp.concatenate([qc] * (2 * H_C), axis=0)
    r = lax.broadcasted_iota(jnp.int32, qc_rows.shape, 0)
    c = lax.broadcasted_iota(jnp.int32, qc_rows.shape, 1)
    keep = ((r % rc) // t_new == c // (2 * DH_C)) & (r // rc == (c % (2 * DH_C)) // DH_C)
    qc_rows = jnp.where(keep, qc_rows, 0.0)
    row_t = (lax.broadcasted_iota(jnp.int32, (2 * rc, 1), 0) % t_new).astype(F32)
    row_slope = row_const(2 * rc, SLOPES_C)
    ps, l = attend(qc_rows,
                   lambda qb, j: lax.dot_general(qb, head_rows(dk[j]).astype(BF), _NT,
                                                 preferred_element_type=F32),
                   kc_ref[...], row_slope, row_t, None)
    lam = _lambda(lq1, lk1, lq2, lk2, lam_init)
    inv1 = 1.0 / l[0:rc]
    inv2 = lam / l[rc:2 * rc]
    ws = [p[0:rc] * inv1 - p[rc:2 * rc] * inv2 for p in ps]
    oc = jnp.dot(ws[n].astype(BF), pad_new(vc_ref[...]).astype(BF), preferred_element_type=F32)
    for j in range(n):
        oc = oc + jnp.dot(ws[j].astype(BF), head_rows(dv[j]).astype(BF),
                          preferred_element_type=F32)
    oc = fold_heads(oc, 2 * DH_C, H_C)
    for h in range(H_C):
        cols = slice(h * LANES, (h + 1) * LANES)
        o = oc[:, cols]
        o = o * lax.rsqrt(jnp.mean(o * o, axis=-1, keepdims=True) + LN_EPS) * ng_ref[...]
        oc_ref[:, cols] = o * (1.0 - lam_init)


def _sample_attn(layer, page_table, qa, ka, va, qc, kc, vc, pools, lam_vecs, norm_g, lam_init):
    n_seq = page_table.shape[0]
    t_new = qa.shape[0] // n_seq
    pt_flat = page_table.reshape(-1)

    def new_spec(w):
        return pl.BlockSpec((t_new, w), lambda b, pt: (b, 0))

    def page_spec(shape, j):
        return pl.BlockSpec((None, None) + shape,
                            lambda b, pt, j=j: (layer, pt[b * N_PAGES + j], 0, 0))

    vec = pl.BlockSpec((1, DH_C), lambda b, pt: (0, 0))
    in_specs = [new_spec(MIX_A)] * 3 + [new_spec(MIX_C)] * 3 + [vec] * 4
    in_specs.append(pl.BlockSpec((1, 2 * DH_C), lambda b, pt: (0, 0)))
    args = [qa, ka, va, qc, kc, vc, *lam_vecs, norm_g.reshape(1, 2 * DH_C)]
    for pool in pools:
        in_specs += [page_spec(pool.shape[2:], j) for j in range(N_PAGES)]
        args += [pool] * N_PAGES
    grid_spec = pltpu.PrefetchScalarGridSpec(
        num_scalar_prefetch=1, grid=(n_seq,), in_specs=in_specs,
        out_specs=[new_spec(MIX_A), new_spec(MIX_C)])
    return pl.pallas_call(
        functools.partial(_sample_attn_kernel, lam_init=lam_init, t_new=t_new),
        grid_spec=grid_spec,
        out_shape=[jax.ShapeDtypeStruct((n_seq * t_new, MIX_A), F32),
                   jax.ShapeDtypeStruct((n_seq * t_new, MIX_C), F32)],
        compiler_params=_params(1),
        name="sample_attn",
    )(pt_flat, *args)


def _conv_kernel(ua_ref, ub_ref, hist_ref, cw_ref, cb_ref, g_ref, b_ref, o_ref, st_ref, z_ref,
                 *, n_seq, t_len, chunk):
    pad = hist_ref.shape[1]
    lead = pad - (CONV_W - 1)
    for s in range(n_seq):
        rows = slice(s * t_len, (s + 1) * t_len)
        z_ref[0:pad, :] = hist_ref[s]
        z_ref[pad:pad + t_len, :] = ua_ref[rows, :] * _sigmoid(ub_ref[rows, :])
        for c0 in range(0, t_len, chunk):
            acc = jnp.zeros((chunk, C_B), F32) + cb_ref[...]
            for w in range(CONV_W):
                lo = c0 + lead + w
                acc = acc + z_ref[lo:lo + chunk, :] * cw_ref[w:w + 1, :]
            y = _layernorm(acc, g_ref[...], b_ref[...])
            o_ref[s * t_len + c0:s * t_len + c0 + chunk, :] = y * _sigmoid(y)
        st_ref[s] = z_ref[lead + t_len:pad + t_len, :]


def _conv_module(proj, row0, n_seq_total, t_len, seq_per_step, hist, conv_w, conv_b, ln_g, ln_b,
                 name):
    rows = seq_per_step * t_len
    blk0 = row0 // rows
    pad = hist.shape[1]
    vec = pl.BlockSpec((1, C_B), lambda i: (0, 0))
    return pl.pallas_call(
        functools.partial(_conv_kernel, n_seq=seq_per_step, t_len=t_len, chunk=min(t_len, 256)),
        grid=(n_seq_total // seq_per_step,),
        in_specs=[pl.BlockSpec((rows, C_B), lambda i: (blk0 + i, COL_UA // C_B)),
                  pl.BlockSpec((rows, C_B), lambda i: (blk0 + i, COL_UB // C_B)),
                  pl.BlockSpec((seq_per_step, pad, C_B), lambda i: (i, 0, 0)),
                  pl.BlockSpec((CONV_W, C_B), lambda i: (0, 0)),
                  vec, vec, vec],
        out_specs=[pl.BlockSpec((rows, C_B), lambda i: (i, 0)),
                   pl.BlockSpec((seq_per_step, CONV_W - 1, C_B), lambda i: (i, 0, 0))],
        out_shape=[jax.ShapeDtypeStruct((n_seq_total * t_len, C_B), F32),
                   jax.ShapeDtypeStruct((n_seq_total, CONV_W - 1, C_B), F32)],
        scratch_shapes=[pltpu.VMEM((pad + t_len, C_B), F32)],
        compiler_params=_params(1),
        name=name,
    )(proj, proj, hist, conv_w, conv_b.reshape(1, C_B), ln_g.reshape(1, C_B), ln_b.reshape(1, C_B))


def _merge_kernel(oa_ref, ob_ref, oc_ref, g0_ref, g1_ref, g2_ref, x_ref, wa_ref, wb_ref, wc_ref,
                  wo_ref, g_ref, b_ref, o_ref):
    def branch(o_r, w_r, gate_r):
        y = jnp.dot(o_r[...].astype(BF), w_r[...], preferred_element_type=F32)
        return _sigmoid(gate_r[...]) * y

    merged = branch(oa_ref, wa_ref, g0_ref) + branch(ob_ref, wb_ref, g1_ref) + branch(oc_ref, wc_ref, g2_ref)
    y = jnp.dot(merged.astype(BF), wo_ref[...], preferred_element_type=F32)
    o_ref[...] = _layernorm(DN_ALPHA * x_ref[...] + y, g_ref[...], b_ref[...])


def _merge(oa, ob, oc, proj, x, wa, wb, wc, wo, g, b, *, tm):
    m = x.shape[0]
    d = D_MODEL

    def rows(w, col=0):
        return pl.BlockSpec((tm, w), lambda i, col=col: (i, col))

    def whole(a):
        return pl.BlockSpec(a.shape, lambda i: (0, 0))

    vec = pl.BlockSpec((1, d), lambda i: (0, 0))
    return pl.pallas_call(
        _merge_kernel,
        grid=(pl.cdiv(m, tm),),
        in_specs=[rows(MIX_A), rows(C_B), rows(MIX_C), rows(d, 0), rows(d, 1), rows(d, 2), rows(d),
                  whole(wa), whole(wb), whole(wc), whole(wo), vec, vec],
        out_specs=rows(d),
        out_shape=jax.ShapeDtypeStruct((m, d), F32),
        compiler_params=_params(1),
        name="merge",
    )(oa, ob, oc, proj, proj, proj, x, wa, wb, wc, wo, g.reshape(1, d), b.reshape(1, d))


def _xattn_kernel(q_ref, mk_ref, mv_ref, o_ref, *, n_seq, t_len, interleaved):
    def head(ref, s, h):
        if interleaved:
            tiles = XDH // LANES
            return jnp.concatenate(
                [ref[s, pl.ds(c * XH + h, MEM_LEN, stride=XH * tiles), :] for c in range(tiles)],
                axis=1).astype(BF)
        return ref[s, :, h * XDH:(h + 1) * XDH].astype(BF)

    for s in range(n_seq):
        rows = slice(s * t_len, (s + 1) * t_len)
        for h in range(XH):
            cols = slice(h * XDH, (h + 1) * XDH)
            qh = (q_ref[rows, cols] * (XDH ** -0.5)).astype(BF)
            sc = lax.dot_general(qh, head(mk_ref, s, h), _NT, preferred_element_type=F32)
            m = jnp.max(sc, axis=-1, keepdims=True)
            p = jnp.exp(sc - m)
            l = jnp.sum(p, axis=-1, keepdims=True)
            o = jnp.dot(p.astype(BF), head(mv_ref, s, h), preferred_element_type=F32)
            o_ref[rows, cols] = o / l


def _xattn(q, row0, n_rows, mem_k, mem_v, layer, seq_per_step, t_len, steps_per_mem, name):
    rows = seq_per_step * t_len
    blk0 = row0 // rows
    interleaved = mem_k.ndim == 4
    if interleaved:
        mem_spec = pl.BlockSpec((None, seq_per_step) + mem_k.shape[2:],
                                lambda i: (layer, i // steps_per_mem, 0, 0))
    else:
        mem_spec = pl.BlockSpec((seq_per_step, MEM_LEN, D_MODEL), lambda i: (i // steps_per_mem, 0, 0))
    return pl.pallas_call(
        functools.partial(_xattn_kernel, n_seq=seq_per_step, t_len=t_len, interleaved=interleaved),
        grid=(n_rows // rows,),
        in_specs=[pl.BlockSpec((rows, D_MODEL), lambda i: (blk0 + i, 0)), mem_spec, mem_spec],
        out_specs=pl.BlockSpec((rows, D_MODEL), lambda i: (i, 0)),
        out_shape=jax.ShapeDtypeStruct((n_rows, D_MODEL), F32),
        compiler_params=_params(1),
        name=name,
    )(q, mem_k, mem_v)


def _router_kernel(x_ref, w_ref, b_ref, idx_ref, gate_ref):
    logits = jnp.dot(x_ref[...], w_ref[...], precision=lax.Precision.HIGHEST,
                     preferred_element_type=F32) + b_ref[...]
    lane = lax.broadcasted_iota(jnp.int32, logits.shape, 1)
    logits = jnp.where(lane < N_EXPERTS, logits, _NEG)
    vals, idxs, _ = _top_picks(logits, lane, TOP_K)
    es = [jnp.exp(v - vals[0]) for v in vals]
    denom = es[0]
    for e in es[1:]:
        denom = denom + e
    idx_out = jnp.zeros(logits.shape, jnp.int32)
    gate_out = jnp.zeros(logits.shape, F32)
    for k in range(TOP_K):
        idx_out = jnp.where(lane == k, idxs[k].astype(jnp.int32), idx_out)
        gate_out = jnp.where(lane == k, es[k] / denom, gate_out)
    idx_ref[...] = idx_out
    gate_ref[...] = gate_out


def _router(x, w_pad, b_pad, *, tm):
    m, d = x.shape
    return pl.pallas_call(
        _router_kernel,
        grid=(pl.cdiv(m, tm),),
        in_specs=[pl.BlockSpec((tm, d), lambda i: (i, 0)),
                  pl.BlockSpec((d, LANES), lambda i: (0, 0)),
                  pl.BlockSpec((1, LANES), lambda i: (0, 0))],
        out_specs=[pl.BlockSpec((tm, LANES), lambda i: (i, 0))] * 2,
        out_shape=[jax.ShapeDtypeStruct((m, LANES), jnp.int32),
                   jax.ShapeDtypeStruct((m, LANES), F32)],
        compiler_params=_params(1),
        name="router",
    )(x, w_pad, b_pad)


def _experts_kernel(be_ref, nv_ref, x_ref, wgu_ref, bgu_ref, wd_ref, bd_ref, o_ref, wgu_b, wd_b):
    i = pl.program_id(0)
    e = be_ref[i]
    prev = be_ref[jnp.maximum(i - 1, 0)]
    live = i < nv_ref[0]

    @pl.when(live & ((i == 0) | (e != prev)))
    def _():
        wgu_b[...] = wgu_ref[...].astype(BF)
        wd_b[...] = wd_ref[...].astype(BF)

    @pl.when(live)
    def _():
        h = jnp.dot(x_ref[...].astype(BF), wgu_b[...], preferred_element_type=F32) + bgu_ref[...]
        hg = jnp.minimum(h[:, :D_FF], SWIGLU_LIMIT)
        hl = jnp.clip(h[:, D_FF:], -SWIGLU_LIMIT, SWIGLU_LIMIT)
        a = hg * _sigmoid(SWIGLU_ALPHA * hg) * (hl + 1.0)
        o_ref[...] = jnp.dot(a.astype(BF), wd_b[...], preferred_element_type=F32) + bd_ref[...]

    @pl.when(jnp.logical_not(live))
    def _():
        o_ref[...] = jnp.zeros(o_ref.shape, F32)


def _experts(layer, blk_e, n_live, xb, w_gate_up, b_gate_up, w_down, b_down):
    cap, d = xb.shape
    n_blocks = cap // MOE_BLOCK
    grid_spec = pltpu.PrefetchScalarGridSpec(
        num_scalar_prefetch=2, grid=(n_blocks,),
        in_specs=[pl.BlockSpec((MOE_BLOCK, d), lambda i, be, nv: (i, 0)),
                  pl.BlockSpec((None, None, d, 2 * D_FF), lambda i, be, nv: (layer, be[i], 0, 0)),
                  pl.BlockSpec((None, None, 1, 2 * D_FF), lambda i, be, nv: (layer, be[i], 0, 0)),
                  pl.BlockSpec((None, None, D_FF, d), lambda i, be, nv: (layer, be[i], 0, 0)),
                  pl.BlockSpec((None, None, 1, d), lambda i, be, nv: (layer, be[i], 0, 0))],
        out_specs=pl.BlockSpec((MOE_BLOCK, d), lambda i, be, nv: (i, 0)),
        scratch_shapes=[pltpu.VMEM((d, 2 * D_FF), BF), pltpu.VMEM((D_FF, d), BF)])
    return pl.pallas_call(
        _experts_kernel,
        grid_spec=grid_spec,
        out_shape=jax.ShapeDtypeStruct((cap, d), F32),
        compiler_params=_params(1),
        name="experts",
    )(blk_e, n_live, xb, w_gate_up, b_gate_up.reshape(DEPTH, N_EXPERTS, 1, 2 * D_FF),
      w_down, b_down.reshape(DEPTH, N_EXPERTS, 1, d))


def _moe(layer, h, router_w_pad, router_b_pad, w_gate_up, b_gate_up, w_down, b_down):
    n, d = h.shape
    idx, gate = _router(h, router_w_pad, router_b_pad, tm=512)
    top_i = idx[:, :TOP_K]
    top_g = gate[:, :TOP_K]
    n_flat = n * TOP_K
    key_bits = (n_flat - 1).bit_length()
    flat_e = top_i.reshape(-1)
    flat_i = jnp.arange(n_flat, dtype=jnp.int32)
    sorted_keys = jnp.sort(flat_e * (1 << key_bits) + flat_i)
    st = (sorted_keys & ((1 << key_bits) - 1)) // TOP_K
    onehot = (flat_e[:, None] == jnp.arange(N_EXPERTS, dtype=jnp.int32)[None, :]).astype(jnp.int32)
    ranks = jnp.cumsum(onehot, axis=0)
    counts = ranks[-1]
    padded = (counts + MOE_BLOCK - 1) // MOE_BLOCK * MOE_BLOCK
    pad_end = jnp.cumsum(padded)
    pad_start = pad_end - padded
    start = jnp.cumsum(counts) - counts
    n_blocks = n_flat // MOE_BLOCK + N_EXPERTS
    cap = n_blocks * MOE_BLOCK
    blk_e = jnp.minimum(jnp.searchsorted(pad_end, jnp.arange(n_blocks, dtype=jnp.int32) * MOE_BLOCK,
                                         side='right'), N_EXPERTS - 1).astype(jnp.int32)
    n_live = (pad_end[-1:] // MOE_BLOCK).astype(jnp.int32)
    row = jnp.arange(cap, dtype=jnp.int32)
    row_e = blk_e[row // MOE_BLOCK]
    row_off = row - pad_start[row_e]
    row_ok = (row_off < counts[row_e]) & (row < pad_end[-1])
    buf_t = jnp.where(row_ok, st[jnp.clip(start[row_e] + row_off, 0, n_flat - 1)], 0)
    pos = (pad_start[flat_e] + jnp.sum(ranks * onehot, axis=1) - 1).reshape(n, TOP_K)
    xb = h[buf_t]
    yb = _experts(layer, blk_e, n_live, xb, w_gate_up, b_gate_up, w_down, b_down)
    y = yb[pos[:, 0]] * top_g[:, 0:1]
    for k in range(1, TOP_K):
        y = y + yb[pos[:, k]] * top_g[:, k:k + 1]
    return y


def kernel(x_prompt, x_sample, mem_prompt, cache_moba_k, cache_moba_v, cache_diff_k, cache_diff_v, cache_mem_k, cache_mem_v, state_conv, page_table, w_in, b_in, conv_w, conv_b, conv_ln_g, conv_ln_b, lam_q1, lam_k1, lam_q2, lam_k2, diff_norm_g, w_branch, w_out, ln1_g, ln1_b, xq_w, xk_w, xv_w, xo_w, ln2_g, ln2_b, router_w, router_b, w_gate_up, b_gate_up, w_down, b_down, ln3_g, ln3_b):
    bp, seq, d = x_prompt.shape
    db, dseq, _ = x_sample.shape
    n_p = bp * seq
    n_s = db * dseq
    n_pool = cache_moba_k.shape[1]
    hist_pad = 32

    o_qa, o_ka, o_va = 0, MIX_A, 2 * MIX_A
    o_u = 3 * MIX_A
    o_qc = o_u + 2 * C_B
    o_kc, o_vc = o_qc + MIX_C, o_qc + 2 * MIX_C
    o_g = o_qc + 3 * MIX_C

    def permute_cols(a):
        return jnp.concatenate([a[..., o_g:], a[..., o_qc:o_g], a[..., o_qa:o_u], a[..., o_u:o_qc]], axis=-1)

    pools = [jnp.transpose(c, (0, 1, 3, 4, 2)).reshape(DEPTH, n_pool, MIX_A, PAGE_SIZE)
             for c in (cache_moba_k, cache_moba_v)]
    pools += [c.reshape(DEPTH, n_pool, PAGE_SIZE * H_C, 2 * DH_C) for c in (cache_diff_k, cache_diff_v)]
    def mem_rows(c):
        c = c.reshape(DEPTH, db, MEM_LEN, XH, XDH // LANES, LANES)
        return jnp.transpose(c, (0, 1, 2, 4, 3, 5)).reshape(DEPTH, db, MEM_LEN * XH * XDH // LANES, LANES)

    mem_k_s = mem_rows(cache_mem_k)
    mem_v_s = mem_rows(cache_mem_v)
    mem2d = mem_prompt.reshape(bp * MEM_LEN, d)
    hist_p = jnp.zeros((bp, hist_pad, C_B), F32)
    hist_s = jnp.pad(state_conv, ((0, 0), (0, 0), (hist_pad - (CONV_W - 1), 0), (0, 0)))
    router_w_pad = jnp.pad(router_w, ((0, 0), (0, 0), (0, LANES - N_EXPERTS)))
    router_b_pad = jnp.pad(router_b, ((0, 0), (0, LANES - N_EXPERTS))).reshape(DEPTH, 1, LANES)

    x = jnp.concatenate([x_prompt.reshape(n_p, d), x_sample.reshape(n_s, d)], axis=0)
    outs = {k: [] for k in ("ka_p", "va_p", "kc_p", "vc_p", "mk_p", "mv_p", "cs_p",
                            "ka_s", "va_s", "kc_s", "vc_s", "cs_s")}

    for l in range(DEPTH):
        lam_init = 0.8 - 0.6 * math.exp(-0.3 * l)
        lam_vecs = [v[l].reshape(1, DH_C) for v in (lam_q1, lam_k1, lam_q2, lam_k2)]
        w_in_l = permute_cols(w_in[l]).astype(BF)
        b_in_l = permute_cols(b_in[l])
        wa = w_branch[l, :MIX_A].astype(BF)
        wb = w_branch[l, MIX_A:MIX_A + C_B].astype(BF)
        wc = w_branch[l, MIX_A + C_B:].astype(BF)

        proj = _matmul(x, w_in_l, b_in_l, tm=512, tn=IN_W // 2, name="proj_in")
        proj_s = proj[n_p:]
        new = {k: proj_s[:, c:c + w] for k, c, w in
               (("qa", COL_QA, MIX_A), ("ka", COL_KA, MIX_A), ("va", COL_VA, MIX_A),
                ("qc", COL_QC, MIX_C), ("kc", COL_KC, MIX_C), ("vc", COL_VC, MIX_C))}

        oa_p = _moba_prompt(proj, bp, seq)
        oc_p = _diff_prompt(proj, lam_vecs, diff_norm_g[l], bp, seq, lam_init)
        oa_s, oc_s = _sample_attn(l, page_table, new["qa"], new["ka"], new["va"],
                                  new["qc"], new["kc"], new["vc"], pools, lam_vecs,
                                  diff_norm_g[l], lam_init)
        ob_p, cs_p = _conv_module(proj, 0, bp, seq, 1, hist_p, conv_w[l], conv_b[l],
                                  conv_ln_g[l], conv_ln_b[l], "conv_prompt")
        ob_s, cs_s = _conv_module(proj, n_p, db, dseq, 16, hist_s[l], conv_w[l], conv_b[l],
                                  conv_ln_g[l], conv_ln_b[l], "conv_sample")
        oa = jnp.concatenate([oa_p, oa_s], axis=0)
        ob = jnp.concatenate([ob_p, ob_s], axis=0)
        oc = jnp.concatenate([oc_p, oc_s], axis=0)
        h1 = _merge(oa, ob, oc, proj, x, wa, wb, wc, w_out[l].astype(BF), ln1_g[l], ln1_b[l], tm=256)

        mk_p = _matmul(mem2d, xk_w[l].astype(BF), tm=512, tn=d, name="mem_k")
        mv_p = _matmul(mem2d, xv_w[l].astype(BF), tm=512, tn=d, name="mem_v")
        q = _matmul(h1, xq_w[l].astype(BF), tm=1024, tn=d, name="xattn_q")
        xo_p = _xattn(q, 0, n_p, mk_p.reshape(bp, MEM_LEN, d), mv_p.reshape(bp, MEM_LEN, d),
                      l, 1, 512, seq // 512, "xattn_prompt")
        xo_s = _xattn(q, n_p, n_s, mem_k_s, mem_v_s, l, 4, dseq, 1, "xattn_sample")
        xo = jnp.concatenate([xo_p, xo_s], axis=0)
        h2 = _mm_res_ln(xo, xo_w[l].astype(BF), h1, ln2_g[l], ln2_b[l], tm=512, name="xattn_out")

        y = _moe(l, h2, router_w_pad[l], router_b_pad[l], w_gate_up, b_gate_up, w_down, b_down)
        x = _res_ln(y, h2, ln3_g[l], ln3_b[l], tm=512, name="moe_out")

        proj_p = proj[:n_p]
        outs["ka_p"].append(proj_p[:, COL_KA:COL_KA + MIX_A].reshape(bp, seq, H_A, DH_A))
        outs["va_p"].append(proj_p[:, COL_VA:COL_VA + MIX_A].reshape(bp, seq, H_A, DH_A))
        outs["kc_p"].append(proj_p[:, COL_KC:COL_KC + MIX_C].reshape(bp, seq, H_C, 2 * DH_C))
        outs["vc_p"].append(proj_p[:, COL_VC:COL_VC + MIX_C].reshape(bp, seq, H_C, 2 * DH_C))
        outs["mk_p"].append(mk_p.reshape(bp, MEM_LEN, XH, XDH))
        outs["mv_p"].append(mv_p.reshape(bp, MEM_LEN, XH, XDH))
        outs["cs_p"].append(cs_p)
        outs["ka_s"].append(new["ka"].reshape(db, dseq, H_A, DH_A))
        outs["va_s"].append(new["va"].reshape(db, dseq, H_A, DH_A))
        outs["kc_s"].append(new["kc"].reshape(db, dseq, H_C, 2 * DH_C))
        outs["vc_s"].append(new["vc"].reshape(db, dseq, H_C, 2 * DH_C))
        outs["cs_s"].append(cs_s)

    return (x[:n_p].reshape(bp, seq, d), x[n_p:].reshape(db, dseq, d),
            jnp.stack(outs["ka_p"]), jnp.stack(outs["va_p"]), jnp.stack(outs["kc_p"]),
            jnp.stack(outs["vc_p"]), jnp.stack(outs["mk_p"]), jnp.stack(outs["mv_p"]),
            jnp.stack(outs["cs_p"]), jnp.stack(outs["ka_s"]), jnp.stack(outs["va_s"]),
            jnp.stack(outs["kc_s"]), jnp.stack(outs["vc_s"]), jnp.stack(outs["cs_s"]))
```

```python
import functools
import math

import jax
import jax.numpy as jnp
from jax import lax
from jax.experimental import pallas as pl
from jax.experimental.pallas import tpu as pltpu

D_MODEL = 1024
DEPTH = 2
PAST_LEN = 2048
PAGE_SIZE = 128
N_PAGES = PAST_LEN // PAGE_SIZE

H_A = 4
DH_A = 64
MOBA_BLOCK = 256
MOBA_TOPK = 3
C_B = 256
CONV_W = 31
H_C = 4
DH_C = 64
MEM_LEN = 256
XH = 4
XDH = D_MODEL // XH
N_EXPERTS = 32
TOP_K = 4
D_FF = D_MODEL
SWIGLU_LIMIT = 7.0
SWIGLU_ALPHA = 1.702
MIX_A = H_A * DH_A
MIX_C = H_C * 2 * DH_C
N_BRANCH = 3
DN_ALPHA = (2 * DEPTH) ** 0.25
LN_EPS = 1e-5
LOG2E = math.log2(math.e)

SLOPES_C = tuple(2.0 ** -(i + 1) for i in range(0, 4))
SLOPES_A = tuple(2.0 ** -(i + 1) for i in range(4, 8))

COL_G = 0
COL_QC = 3 * D_MODEL
COL_KC = COL_QC + MIX_C
COL_VC = COL_KC + MIX_C
COL_QA = COL_VC + MIX_C
COL_KA = COL_QA + MIX_A
COL_VA = COL_KA + MIX_A
COL_UA = COL_VA + MIX_A
COL_UB = COL_UA + C_B
IN_W = COL_UB + C_B

LANES = 128
SUBLANES = 8
Q_TILE = MOBA_BLOCK
KEY_TILE = MOBA_BLOCK
MOE_BLOCK = 256
VMEM_LIMIT = 56 * 1024 * 1024

_NT = (((1,), (1,)), ((), ()))
_NEG = -jnp.inf
BF = jnp.bfloat16
F32 = jnp.float32


def _params(n_grid):
    return pltpu.CompilerParams(dimension_semantics=("arbitrary",) * n_grid,
                                vmem_limit_bytes=VMEM_LIMIT)


def _sigmoid(x):
    return 1.0 / (1.0 + jnp.exp(-x))


def _layernorm(x, g, b):
    mu = jnp.mean(x, axis=-1, keepdims=True)
    xc = x - mu
    var = jnp.mean(xc * xc, axis=-1, keepdims=True)
    return xc * lax.rsqrt(var + LN_EPS) * g + b


def _top_picks(g, pos, k, axis=-1):
    lane = pos.astype(F32)
    sel = jnp.zeros(g.shape, F32)
    vals, idxs = [], []
    for _ in range(k):
        m = jnp.max(g, axis=axis, keepdims=True)
        cand = jnp.where(g == m, lane, float(g.shape[axis]))
        idx = jnp.where(m > _NEG, jnp.min(cand, axis=axis, keepdims=True), -1.0)
        pick = lane == idx
        sel = jnp.where(pick, 1.0, sel)
        g = jnp.where(pick, _NEG, g)
        vals.append(m)
        idxs.append(idx)
    return vals, idxs, sel


def _mm_kernel(x_ref, w_ref, *rest, has_bias):
    o_ref = rest[-1]
    acc = jnp.dot(x_ref[...].astype(BF), w_ref[...], preferred_element_type=F32)
    if has_bias:
        acc = acc + rest[0][...]
    o_ref[...] = acc


def _matmul(x, w, b=None, *, tm, tn, name):
    m, k = x.shape
    n = w.shape[1]
    in_specs = [pl.BlockSpec((tm, k), lambda j, i: (i, 0)),
                pl.BlockSpec((k, tn), lambda j, i: (0, j))]
    args = [x, w]
    if b is not None:
        in_specs.append(pl.BlockSpec((1, tn), lambda j, i: (0, j)))
        args.append(b.reshape(1, n))
    return pl.pallas_call(
        functools.partial(_mm_kernel, has_bias=b is not None),
        grid=(n // tn, pl.cdiv(m, tm)),
        in_specs=in_specs,
        out_specs=pl.BlockSpec((tm, tn), lambda j, i: (i, j)),
        out_shape=jax.ShapeDtypeStruct((m, n), F32),
        compiler_params=_params(2),
        name=name,
    )(*args)


def _mm_res_ln_kernel(x_ref, w_ref, r_ref, g_ref, b_ref, o_ref):
    y = jnp.dot(x_ref[...].astype(BF), w_ref[...], preferred_element_type=F32)
    o_ref[...] = _layernorm(DN_ALPHA * r_ref[...] + y, g_ref[...], b_ref[...])


def _mm_res_ln(x, w, res, g, b, *, tm, name):
    m, k = x.shape
    n = w.shape[1]
    return pl.pallas_call(
        _mm_res_ln_kernel,
        grid=(pl.cdiv(m, tm),),
        in_specs=[pl.BlockSpec((tm, k), lambda i: (i, 0)),
                  pl.BlockSpec((k, n), lambda i: (0, 0)),
                  pl.BlockSpec((tm, n), lambda i: (i, 0)),
                  pl.BlockSpec((1, n), lambda i: (0, 0)),
                  pl.BlockSpec((1, n), lambda i: (0, 0))],
        out_specs=pl.BlockSpec((tm, n), lambda i: (i, 0)),
        out_shape=jax.ShapeDtypeStruct((m, n), F32),
        compiler_params=_params(1),
        name=name,
    )(x, w, res, g.reshape(1, n), b.reshape(1, n))


def _res_ln_kernel(y_ref, r_ref, g_ref, b_ref, o_ref):
    o_ref[...] = _layernorm(DN_ALPHA * r_ref[...] + y_ref[...], g_ref[...], b_ref[...])


def _res_ln(y, res, g, b, *, tm, name):
    m, n = y.shape
    return pl.pallas_call(
        _res_ln_kernel,
        grid=(pl.cdiv(m, tm),),
        in_specs=[pl.BlockSpec((tm, n), lambda i: (i, 0)),
                  pl.BlockSpec((tm, n), lambda i: (i, 0)),
                  pl.BlockSpec((1, n), lambda i: (0, 0)),
                  pl.BlockSpec((1, n), lambda i: (0, 0))],
        out_specs=pl.BlockSpec((tm, n), lambda i: (i, 0)),
        out_shape=jax.ShapeDtypeStruct((m, n), F32),
        compiler_params=_params(1),
        name=name,
    )(y, res, g.reshape(1, n), b.reshape(1, n))


def _flash_block(chains, carry):
    ss = []
    for k, q, bias, mask, _, _ in chains:
        s = lax.dot_general(k, q, _NT, preferred_element_type=F32) + bias
        ss.append(s if mask is None else jnp.where(mask, s, _NEG))
    stats = []
    for i, (_, _, _, _, shift, _) in enumerate(chains):
        top = jnp.max(ss[i], axis=0, keepdims=True)
        if carry is None:
            m_new, a = top, None
            p = jnp.exp2(ss[i] - m_new)
            l = jnp.sum(p, axis=0, keepdims=True)
        else:
            m, l_old, _ = carry[i]
            m_new = jnp.maximum(m, top - shift)
            a = jnp.exp2(m - m_new)
            p = jnp.exp2(ss[i] - (m_new + shift))
            l = a * l_old + jnp.sum(p, axis=0, keepdims=True)
        stats.append((m_new, l, a, p.astype(BF)))
    out = []
    for i, (_, _, _, _, _, v_t) in enumerate(chains):
        m_new, l, a, p = stats[i]
        pv = jnp.dot(v_t, p, preferred_element_type=F32)
        out.append((m_new, l, pv if carry is None else a * carry[i][2] + pv))
    return tuple(out)


def _tile_geometry():
    key_j = lax.broadcasted_iota(jnp.int32, (KEY_TILE, Q_TILE), 0)
    qry_i = lax.broadcasted_iota(jnp.int32, (KEY_TILE, Q_TILE), 1)
    causal = qry_i >= key_j
    rel = (key_j - (KEY_TILE - 1)).astype(F32)
    return causal, rel


def _stage_kv(k_ref, v_ref, kb_ref, vt_ref, n_blocks):
    kb_ref[...] = k_ref[...].astype(BF)
    for n in range(n_blocks):
        vt_ref[n] = v_ref[n * KEY_TILE:(n + 1) * KEY_TILE, :].T.astype(BF)


def _moba_prompt_kernel(q_ref, k_ref, v_ref, o_ref, kb_ref, vt_ref, km_ref, *, n_blocks):
    cur = pl.program_id(1)

    @pl.when(cur == 0)
    def _():
        _stage_kv(k_ref, v_ref, kb_ref, vt_ref, n_blocks)
        km_ref[...] = jnp.zeros(km_ref.shape, F32)
        for n in range(n_blocks):
            km_ref[n:n + 1, :] = jnp.mean(k_ref[n * MOBA_BLOCK:(n + 1) * MOBA_BLOCK, :],
                                          axis=0, keepdims=True)

    causal, rel = _tile_geometry()
    q = q_ref[...]
    lane_q = lax.broadcasted_iota(jnp.int32, q.shape, 1)
    lane_p = lax.broadcasted_iota(jnp.int32, (Q_TILE, LANES), 1)
    blk_row = lax.broadcasted_iota(jnp.int32, (km_ref.shape[0], Q_TILE), 0)

    consts = []
    for h in range(H_A):
        pair, hh = divmod(h, 2)
        cols = slice(pair * LANES, (pair + 1) * LANES)
        slope2 = SLOPES_A[h] * LOG2E
        gate = lax.dot_general(km_ref[...], jnp.where(lane_q // DH_A == h, q, 0.0), _NT,
                               precision=lax.Precision.HIGHEST,
                               preferred_element_type=F32)
        gate = jnp.where(blk_row < cur, gate, _NEG)
        _, _, sel = _top_picks(gate, blk_row, MOBA_TOPK, axis=0)
        qm = jnp.where(lane_p // DH_A == hh, q[:, cols] * (DH_A ** -0.5 * LOG2E), 0.0).astype(BF)
        consts.append((cols, slope2, sel, qm, slope2 * rel))

    def block(n, carry):
        start = pl.multiple_of(n * KEY_TILE, KEY_TILE)
        chains = []
        for cols, slope2, sel, qm, bias in consts:
            if carry is None:
                mask, shift = causal, None
            else:
                mask = jnp.max(jnp.where(blk_row == n, sel, 0.0), axis=0, keepdims=True) > 0.5
                shift = slope2 * jnp.full((1, Q_TILE), (cur - n) * KEY_TILE, jnp.int32).astype(F32)
            chains.append((kb_ref[pl.ds(start, KEY_TILE), cols], qm, bias, mask, shift,
                           vt_ref[n, cols, :]))
        return _flash_block(chains, carry)

    final = lax.fori_loop(0, cur, block, block(cur, None))
    row = lax.broadcasted_iota(jnp.int32, (LANES, Q_TILE), 0)
    for pair in range(H_A // 2):
        (_, l0, a0), (_, l1, a1) = final[2 * pair], final[2 * pair + 1]
        o_t = jnp.where(row < DH_A, a0 / l0, a1 / l1)
        o_ref[:, pair * LANES:(pair + 1) * LANES] = o_t.T


def _moba_prompt(proj, batch, seq, n_total):
    n_q = seq // Q_TILE
    n_blocks = seq // MOBA_BLOCK
    km_rows = -(-n_blocks // SUBLANES) * SUBLANES
    cq, ck, cv = COL_QA // MIX_A, COL_KA // MIX_A, COL_VA // MIX_A
    return pl.pallas_call(
        functools.partial(_moba_prompt_kernel, n_blocks=n_blocks),
        grid=(batch, n_q),
        in_specs=[pl.BlockSpec((Q_TILE, MIX_A), lambda b, i: (b * n_q + i, cq)),
                  pl.BlockSpec((seq, MIX_A), lambda b, i: (b, ck)),
                  pl.BlockSpec((seq, MIX_A), lambda b, i: (b, cv))],
        out_specs=pl.BlockSpec((Q_TILE, MIX_A), lambda b, i: (b * n_q + i, 0)),
        out_shape=jax.ShapeDtypeStruct((n_total, MIX_A), F32),
        scratch_shapes=[pltpu.VMEM((seq, MIX_A), BF),
                        pltpu.VMEM((n_blocks, MIX_A, KEY_TILE), BF),
                        pltpu.VMEM((km_rows, MIX_A), F32)],
        compiler_params=_params(2),
        name="moba_prompt",
    )(proj, proj, proj)


def _lambda(lq1, lk1, lq2, lk2, lam_init):
    return (jnp.exp(jnp.sum(lq1[...] * lk1[...], axis=-1, keepdims=True))
            - jnp.exp(jnp.sum(lq2[...] * lk2[...], axis=-1, keepdims=True)) + lam_init)


def _diff_prompt_kernel(q_ref, k_ref, v_ref, lq1, lk1, lq2, lk2, ng_ref, o_ref, kb_ref, vt_ref,
                        *, lam_init, n_blocks):
    cur = pl.program_id(1)

    @pl.when(cur == 0)
    def _():
        _stage_kv(k_ref, v_ref, kb_ref, vt_ref, n_blocks)

    causal, rel = _tile_geometry()
    lam = _lambda(lq1, lk1, lq2, lk2, lam_init)
    lane = lax.broadcasted_iota(jnp.int32, (Q_TILE, LANES), 1)

    consts = []
    for h in range(H_C):
        cols = slice(h * LANES, (h + 1) * LANES)
        slope2 = SLOPES_C[h] * LOG2E
        qh = q_ref[:, cols] * (DH_C ** -0.5 * LOG2E)
        consts.append((cols, slope2, jnp.where(lane < DH_C, qh, 0.0).astype(BF),
                       jnp.where(lane >= DH_C, qh, 0.0).astype(BF), slope2 * rel))

    def block(n, carry):
        start = pl.multiple_of(n * KEY_TILE, KEY_TILE)
        chains = []
        for cols, slope2, q1, q2, bias in consts:
            kb = kb_ref[pl.ds(start, KEY_TILE), cols]
            v_t = vt_ref[n, cols, :]
            if carry is None:
                mask, shift = causal, None
            else:
                mask = None
                shift = slope2 * jnp.full((1, Q_TILE), (cur - n) * KEY_TILE, jnp.int32).astype(F32)
            chains += [(kb, q1, bias, mask, shift, v_t), (kb, q2, bias, mask, shift, v_t)]
        return _flash_block(chains, carry)

    final = lax.fori_loop(0, cur, block, block(cur, None))
    for h in range(H_C):
        (_, l1, a1), (_, l2, a2) = final[2 * h], final[2 * h + 1]
        o_t = a1 / l1 - lam * (a2 / l2)
        o_t = o_t * lax.rsqrt(jnp.mean(o_t * o_t, axis=0, keepdims=True) + LN_EPS) * ng_ref[...]
        o_ref[:, h * LANES:(h + 1) * LANES] = (o_t * (1.0 - lam_init)).T


def _diff_prompt(proj, lam_vecs, norm_g, batch, seq, lam_init, n_total):
    n_q = seq // Q_TILE
    n_blocks = seq // KEY_TILE
    cq, ck, cv = COL_QC // MIX_C, COL_KC // MIX_C, COL_VC // MIX_C
    vec = pl.BlockSpec((1, DH_C), lambda b, i: (0, 0))
    return pl.pallas_call(
        functools.partial(_diff_prompt_kernel, lam_init=lam_init, n_blocks=n_blocks),
        grid=(batch, n_q),
        in_specs=[pl.BlockSpec((Q_TILE, MIX_C), lambda b, i: (b * n_q + i, cq)),
                  pl.BlockSpec((seq, MIX_C), lambda b, i: (b, ck)),
                  pl.BlockSpec((seq, MIX_C), lambda b, i: (b, cv)),
                  vec, vec, vec, vec,
                  pl.BlockSpec((2 * DH_C, 1), lambda b, i: (0, 0))],
        out_specs=pl.BlockSpec((Q_TILE, MIX_C), lambda b, i: (b * n_q + i, 0)),
        out_shape=jax.ShapeDtypeStruct((n_total, MIX_C), F32),
        scratch_shapes=[pltpu.VMEM((seq, MIX_C), BF),
                        pltpu.VMEM((n_blocks, MIX_C, KEY_TILE), BF)],
        compiler_params=_params(2),
        name="diff_prompt",
    )(proj, proj, proj, *lam_vecs, norm_g.reshape(2 * DH_C, 1))


def _sample_attn_kernel(pt_ref, qa_ref, ka_ref, va_ref, qc_ref, kc_ref, vc_ref,
                        lq1, lk1, lq2, lk2, ng_ref, *rest, lam_init, t_new):
    del pt_ref
    n = N_PAGES
    mk, mv = rest[0:n], rest[n:2 * n]
    dk, dv = rest[2 * n:3 * n], rest[3 * n:4 * n]
    oa_ref, oc_ref = rest[-2:]
    pad_rows = PAGE_SIZE - t_new

    def pad_new(x):
        return jnp.concatenate([x, jnp.zeros((pad_rows, x.shape[1]), x.dtype)], axis=0)

    def head_rows(ref):
        return jnp.concatenate([ref[pl.ds(h, PAGE_SIZE, stride=H_C), :] for h in range(H_C)],
                               axis=1)

    def raw_scores(q_rows, page_scores, k_new):
        qb = q_rows.astype(BF)
        tiles = [page_scores(qb, j) for j in range(n)]
        tiles.append(lax.dot_general(qb, pad_new(k_new).astype(BF), _NT, preferred_element_type=F32))
        return tiles

    def softmax_tiles(tiles, slope_col, t_col, gate_sel):
        shape = tiles[0].shape
        lane_f = lax.broadcasted_iota(jnp.int32, shape, 1).astype(F32)
        slope = jnp.broadcast_to(slope_col, shape)
        t_row = jnp.broadcast_to(t_col, shape)
        near = slope * (lane_f - t_row)
        out = []
        for j in range(n):
            s = tiles[j] + (near - slope * float(PAST_LEN - j * PAGE_SIZE))
            if gate_sel is not None:
                s = jnp.where(gate_sel[j // (MOBA_BLOCK // PAGE_SIZE)], s, _NEG)
            out.append(s)
        out.append(jnp.where(lane_f <= t_row, tiles[n] + near, _NEG))
        top = out[0]
        for s in out[1:]:
            top = jnp.maximum(top, s)
        m = jnp.max(top, axis=-1, keepdims=True)
        ps = [jnp.exp(s - m) for s in out]
        tot = ps[0]
        for p in ps[1:]:
            tot = tot + p
        return ps, jnp.sum(tot, axis=-1, keepdims=True)

    def fold_heads(o, head_w, n_heads):
        r = lax.broadcasted_iota(jnp.int32, o.shape, 0)
        c = lax.broadcasted_iota(jnp.int32, o.shape, 1)
        o = jnp.where(r // t_new == c // head_w, o, 0.0)
        out = o[0:t_new]
        for h in range(1, n_heads):
            out = out + o[h * t_new:(h + 1) * t_new]
        return out

    def row_const(rows, values):
        h = lax.broadcasted_iota(jnp.int32, (rows, 1), 0) // t_new
        out = jnp.full((rows, 1), values[-1], F32)
        for i in range(len(values) - 2, -1, -1):
            out = jnp.where(h % len(values) == i, values[i], out)
        return out

    ra = H_A * t_new
    qa = qa_ref[...] * (DH_A ** -0.5)
    qa_rows = jnp.concatenate([qa] * H_A, axis=0)
    r = lax.broadcasted_iota(jnp.int32, qa_rows.shape, 0)
    c = lax.broadcasted_iota(jnp.int32, qa_rows.shape, 1)
    qa_rows = jnp.where(r // t_new == c // DH_A, qa_rows, 0.0)
    row_t = (lax.broadcasted_iota(jnp.int32, (ra, 1), 0) % t_new).astype(F32)
    row_slope = row_const(ra, SLOPES_A)
    pages_per_block = MOBA_BLOCK // PAGE_SIZE
    n_past_blocks = PAST_LEN // MOBA_BLOCK
    lane_m = lax.broadcasted_iota(jnp.int32, (MIX_A, LANES), 1)
    km_t = jnp.zeros((MIX_A, LANES), F32)
    for b in range(n_past_blocks):
        tot = mk[b * pages_per_block][...]
        for j in range(1, pages_per_block):
            tot = tot + mk[b * pages_per_block + j][...]
        mean = jnp.sum(tot, axis=1, keepdims=True) * (1.0 / MOBA_BLOCK)
        km_t = jnp.where(lane_m == b, mean, km_t)
    gate = jnp.dot(qa_rows, km_t, precision=lax.Precision.HIGHEST,
                   preferred_element_type=F32)
    lane_g = lax.broadcasted_iota(jnp.int32, gate.shape, 1)
    gate = jnp.where(lane_g < n_past_blocks, gate, _NEG)
    _, _, sel = _top_picks(gate, lane_g, MOBA_TOPK)
    gate_sel = [jnp.broadcast_to(jnp.max(jnp.where(lane_g == b, sel, 0.0), axis=-1, keepdims=True),
                                 (ra, PAGE_SIZE)) > 0.5 for b in range(n_past_blocks)]

    rc = H_C * t_new
    qc = qc_ref[...] * (DH_C ** -0.5)
    qc_rows = jnp.concatenate([qc] * (2 * H_C), axis=0)
    r = lax.broadcasted_iota(jnp.int32, qc_rows.shape, 0)
    c = lax.broadcasted_iota(jnp.int32, qc_rows.shape, 1)
    keep = ((r % rc) // t_new == c // (2 * DH_C)) & (r // rc == (c % (2 * DH_C)) // DH_C)
    qc_rows = jnp.where(keep, qc_rows, 0.0)
    row_t_c = (lax.broadcasted_iota(jnp.int32, (2 * rc, 1), 0) % t_new).astype(F32)
    row_slope_c = row_const(2 * rc, SLOPES_C)
    lam = _lambda(lq1, lk1, lq2, lk2, lam_init)

    raw_a = raw_scores(qa_rows,
                       lambda qb, j: jnp.dot(qb, mk[j][...].astype(BF), preferred_element_type=F32),
                       ka_ref[...])
    raw_c = raw_scores(qc_rows,
                       lambda qb, j: lax.dot_general(qb, head_rows(dk[j]).astype(BF), _NT,
                                                     preferred_element_type=F32),
                       kc_ref[...])
    ps_a, l_a = softmax_tiles(raw_a, row_slope, row_t, gate_sel)
    ps_c, l_c = softmax_tiles(raw_c, row_slope_c, row_t_c, None)
    inv1 = jnp.broadcast_to(1.0 / l_c[0:rc], (rc, PAGE_SIZE))
    inv2 = jnp.broadcast_to(lam / l_c[rc:2 * rc], (rc, PAGE_SIZE))
    ws = [p[0:rc] * inv1 - p[rc:2 * rc] * inv2 for p in ps_c]
    oa = jnp.dot(ps_a[n].astype(BF), pad_new(va_ref[...]).astype(BF), preferred_element_type=F32)
    for j in range(n):
        oa = oa + lax.dot_general(ps_a[j].astype(BF), mv[j][...].astype(BF), _NT,
                                  preferred_element_type=F32)
    oc = jnp.dot(ws[n].astype(BF), pad_new(vc_ref[...]).astype(BF), preferred_element_type=F32)
    for j in range(n):
        oc = oc + jnp.dot(ws[j].astype(BF), head_rows(dv[j]).astype(BF),
                          preferred_element_type=F32)
    oa_ref[...] = fold_heads(oa / l_a, DH_A, H_A)
    oc = fold_heads(oc, 2 * DH_C, H_C)
    for h in range(H_C):
        cols = slice(h * LANES, (h + 1) * LANES)
        o = oc[:, cols]
        o = o * lax.rsqrt(jnp.mean(o * o, axis=-1, keepdims=True) + LN_EPS) * ng_ref[...]
        oc_ref[:, cols] = o * (1.0 - lam_init)


def _sample_attn(layer, page_table, qa, ka, va, qc, kc, vc, pools, lam_vecs, norm_g, lam_init,
                 oa_all, oc_all, row0):
    n_seq = page_table.shape[0]
    t_new = qa.shape[0] // n_seq
    pt_flat = page_table.reshape(-1)

    def new_spec(w):
        return pl.BlockSpec((t_new, w), lambda b, pt: (b, 0))

    def page_spec(shape, j):
        return pl.BlockSpec((None, None) + shape,
                            lambda b, pt, j=j: (layer, pt[b * N_PAGES + j], 0, 0))

    vec = pl.BlockSpec((1, DH_C), lambda b, pt: (0, 0))
    in_specs = [new_spec(MIX_A)] * 3 + [new_spec(MIX_C)] * 3 + [vec] * 4
    in_specs.append(pl.BlockSpec((1, 2 * DH_C), lambda b, pt: (0, 0)))
    args = [qa, ka, va, qc, kc, vc, *lam_vecs, norm_g.reshape(1, 2 * DH_C)]
    for pool in pools:
        in_specs += [page_spec(pool.shape[2:], j) for j in range(N_PAGES)]
        args += [pool] * N_PAGES
    blk0 = row0 // t_new
    first_alias = 1 + len(args)
    in_specs += [pl.BlockSpec(memory_space=pl.ANY)] * 2
    args += [oa_all, oc_all]
    grid_spec = pltpu.PrefetchScalarGridSpec(
        num_scalar_prefetch=1, grid=(n_seq,), in_specs=in_specs,
        out_specs=[pl.BlockSpec((t_new, MIX_A), lambda b, pt: (blk0 + b, 0)),
                   pl.BlockSpec((t_new, MIX_C), lambda b, pt: (blk0 + b, 0))])
    return pl.pallas_call(
        functools.partial(_sample_attn_kernel, lam_init=lam_init, t_new=t_new),
        grid_spec=grid_spec,
        out_shape=[jax.ShapeDtypeStruct(oa_all.shape, F32), jax.ShapeDtypeStruct(oc_all.shape, F32)],
        input_output_aliases={first_alias: 0, first_alias + 1: 1},
        compiler_params=_params(1),
        name="sample_attn",
    )(pt_flat, *args)


def _conv_kernel(ua_ref, ub_ref, hist_ref, cw_ref, cb_ref, g_ref, b_ref, *rest, n_seq, t_len, chunk):
    o_ref, st_ref, z_ref = rest[-3:]
    pad = hist_ref.shape[1]
    lead = pad - (CONV_W - 1)
    for s in range(n_seq):
        rows = slice(s * t_len, (s + 1) * t_len)
        z_ref[0:pad, :] = hist_ref[s]
        z_ref[pad:pad + t_len, :] = ua_ref[rows, :] * _sigmoid(ub_ref[rows, :])
        for c0 in range(0, t_len, chunk):
            acc = jnp.zeros((chunk, C_B), F32) + cb_ref[...]
            for w in range(CONV_W):
                lo = c0 + lead + w
                acc = acc + z_ref[lo:lo + chunk, :] * cw_ref[w:w + 1, :]
            y = _layernorm(acc, g_ref[...], b_ref[...])
            o_ref[s * t_len + c0:s * t_len + c0 + chunk, :] = y * _sigmoid(y)
        st_ref[s] = z_ref[lead + t_len:pad + t_len, :]


def _conv_module(proj, row0, n_seq_total, t_len, seq_per_step, hist, conv_w, conv_b, ln_g, ln_b,
                 name, into=None):
    rows = seq_per_step * t_len
    blk0 = row0 // rows
    pad = hist.shape[1]
    vec = pl.BlockSpec((1, C_B), lambda i: (0, 0))
    extra_specs = [] if into is None else [pl.BlockSpec(memory_space=pl.ANY)]
    extra_args = [] if into is None else [into]
    return pl.pallas_call(
        functools.partial(_conv_kernel, n_seq=seq_per_step, t_len=t_len, chunk=min(t_len, 256)),
        grid=(n_seq_total // seq_per_step,),
        in_specs=[pl.BlockSpec((rows, C_B), lambda i: (blk0 + i, COL_UA // C_B)),
                  pl.BlockSpec((rows, C_B), lambda i: (blk0 + i, COL_UB // C_B)),
                  pl.BlockSpec((seq_per_step, pad, C_B), lambda i: (i, 0, 0)),
                  pl.BlockSpec((CONV_W, C_B), lambda i: (0, 0)),
                  vec, vec, vec] + extra_specs,
        out_specs=[pl.BlockSpec((rows, C_B), lambda i: (blk0 + i, 0)),
                   pl.BlockSpec((seq_per_step, CONV_W - 1, C_B), lambda i: (i, 0, 0))],
        out_shape=[jax.ShapeDtypeStruct((proj.shape[0], C_B), F32),
                   jax.ShapeDtypeStruct((n_seq_total, CONV_W - 1, C_B), F32)],
        input_output_aliases={} if into is None else {7: 0},
        scratch_shapes=[pltpu.VMEM((pad + t_len, C_B), F32)],
        compiler_params=_params(1),
        name=name,
    )(proj, proj, hist, conv_w, conv_b.reshape(1, C_B), ln_g.reshape(1, C_B), ln_b.reshape(1, C_B),
      *extra_args)


def _merge_kernel(oa_ref, ob_ref, oc_ref, g0_ref, g1_ref, g2_ref, x_ref, wa_ref, wb_ref, wc_ref,
                  wo_ref, g_ref, b_ref, o_ref):
    def branch(o_r, w_r, gate_r):
        y = jnp.dot(o_r[...].astype(BF), w_r[...], preferred_element_type=F32)
        return _sigmoid(gate_r[...]) * y

    merged = branch(oa_ref, wa_ref, g0_ref) + branch(ob_ref, wb_ref, g1_ref) + branch(oc_ref, wc_ref, g2_ref)
    y = jnp.dot(merged.astype(BF), wo_ref[...], preferred_element_type=F32)
    o_ref[...] = _layernorm(DN_ALPHA * x_ref[...] + y, g_ref[...], b_ref[...])


def _merge(oa, ob, oc, proj, x, wa, wb, wc, wo, g, b, *, tm):
    m = x.shape[0]
    d = D_MODEL

    def rows(w, col=0):
        return pl.BlockSpec((tm, w), lambda i, col=col: (i, col))

    def whole(a):
        return pl.BlockSpec(a.shape, lambda i: (0, 0))

    vec = pl.BlockSpec((1, d), lambda i: (0, 0))
    return pl.pallas_call(
        _merge_kernel,
        grid=(pl.cdiv(m, tm),),
        in_specs=[rows(MIX_A), rows(C_B), rows(MIX_C), rows(d, 0), rows(d, 1), rows(d, 2), rows(d),
                  whole(wa), whole(wb), whole(wc), whole(wo), vec, vec],
        out_specs=rows(d),
        out_shape=jax.ShapeDtypeStruct((m, d), F32),
        compiler_params=_params(1),
        name="merge",
    )(oa, ob, oc, proj, proj, proj, x, wa, wb, wc, wo, g.reshape(1, d), b.reshape(1, d))


def _xattn_kernel(q_ref, mk_ref, mv_ref, *rest, n_seq, t_len, interleaved):
    o_ref = rest[-1]
    def head(ref, s, h):
        if interleaved:
            tiles = XDH // LANES
            return jnp.concatenate(
                [ref[s, pl.ds(c * XH + h, MEM_LEN, stride=XH * tiles), :] for c in range(tiles)],
                axis=1).astype(BF)
        return ref[s, :, h * XDH:(h + 1) * XDH].astype(BF)

    for s in range(n_seq):
        rows = slice(s * t_len, (s + 1) * t_len)
        for h in range(XH):
            cols = slice(h * XDH, (h + 1) * XDH)
            qh = (q_ref[rows, cols] * (XDH ** -0.5)).astype(BF)
            sc = lax.dot_general(qh, head(mk_ref, s, h), _NT, preferred_element_type=F32)
            m = jnp.max(sc, axis=-1, keepdims=True)
            p = jnp.exp(sc - m)
            l = jnp.sum(p, axis=-1, keepdims=True)
            o = jnp.dot(p.astype(BF), head(mv_ref, s, h), preferred_element_type=F32)
            o_ref[rows, cols] = o / l


def _xattn(q, row0, n_rows, mem_k, mem_v, layer, seq_per_step, t_len, steps_per_mem, name, into=None):
    rows = seq_per_step * t_len
    blk0 = row0 // rows
    interleaved = mem_k.ndim == 4
    if interleaved:
        mem_spec = pl.BlockSpec((None, seq_per_step) + mem_k.shape[2:],
                                lambda i: (layer, i // steps_per_mem, 0, 0))
    else:
        mem_spec = pl.BlockSpec((seq_per_step, MEM_LEN, D_MODEL), lambda i: (i // steps_per_mem, 0, 0))
    return pl.pallas_call(
        functools.partial(_xattn_kernel, n_seq=seq_per_step, t_len=t_len, interleaved=interleaved),
        grid=(n_rows // rows,),
        in_specs=[pl.BlockSpec((rows, D_MODEL), lambda i: (blk0 + i, 0)), mem_spec, mem_spec]
        + ([] if into is None else [pl.BlockSpec(memory_space=pl.ANY)]),
        out_specs=pl.BlockSpec((rows, D_MODEL), lambda i: (blk0 + i, 0)),
        out_shape=jax.ShapeDtypeStruct(q.shape, F32),
        input_output_aliases={} if into is None else {3: 0},
        compiler_params=_params(1),
        name=name,
    )(q, mem_k, mem_v, *([] if into is None else [into]))


def _router_kernel(x_ref, w_ref, b_ref, idx_ref, gate_ref):
    logits = jnp.dot(x_ref[...], w_ref[...], precision=lax.Precision.HIGHEST,
                     preferred_element_type=F32) + b_ref[...]
    lane = lax.broadcasted_iota(jnp.int32, logits.shape, 1)
    logits = jnp.where(lane < N_EXPERTS, logits, _NEG)
    vals, idxs, _ = _top_picks(logits, lane, TOP_K)
    es = [jnp.exp(v - vals[0]) for v in vals]
    denom = es[0]
    for e in es[1:]:
        denom = denom + e
    idx_out = jnp.zeros(logits.shape, jnp.int32)
    gate_out = jnp.zeros(logits.shape, F32)
    for k in range(TOP_K):
        idx_out = jnp.where(lane == k, idxs[k].astype(jnp.int32), idx_out)
        gate_out = jnp.where(lane == k, es[k] / denom, gate_out)
    idx_ref[...] = idx_out
    gate_ref[...] = gate_out


def _router(x, w_pad, b_pad, *, tm):
    m, d = x.shape
    return pl.pallas_call(
        _router_kernel,
        grid=(pl.cdiv(m, tm),),
        in_specs=[pl.BlockSpec((tm, d), lambda i: (i, 0)),
                  pl.BlockSpec((d, LANES), lambda i: (0, 0)),
                  pl.BlockSpec((1, LANES), lambda i: (0, 0))],
        out_specs=[pl.BlockSpec((tm, LANES), lambda i: (i, 0))] * 2,
        out_shape=[jax.ShapeDtypeStruct((m, LANES), jnp.int32),
                   jax.ShapeDtypeStruct((m, LANES), F32)],
        compiler_params=_params(1),
        name="router",
    )(x, w_pad, b_pad)


def _experts_kernel(be_ref, nv_ref, x_ref, wgu_ref, bgu_ref, wd_ref, bd_ref, o_ref, wgu_b, wd_b):
    i = pl.program_id(0)
    e = be_ref[i]
    prev = be_ref[jnp.maximum(i - 1, 0)]
    live = i < nv_ref[0]

    @pl.when(live & ((i == 0) | (e != prev)))
    def _():
        wgu_b[...] = wgu_ref[...].astype(BF)
        wd_b[...] = wd_ref[...].astype(BF)

    @pl.when(live)
    def _():
        h = jnp.dot(x_ref[...].astype(BF), wgu_b[...], preferred_element_type=F32) + bgu_ref[...]
        hg = jnp.minimum(h[:, :D_FF], SWIGLU_LIMIT)
        hl = jnp.clip(h[:, D_FF:], -SWIGLU_LIMIT, SWIGLU_LIMIT)
        a = hg * _sigmoid(SWIGLU_ALPHA * hg) * (hl + 1.0)
        o_ref[...] = jnp.dot(a.astype(BF), wd_b[...], preferred_element_type=F32) + bd_ref[...]

    @pl.when(jnp.logical_not(live))
    def _():
        o_ref[...] = jnp.zeros(o_ref.shape, F32)


def _experts(layer, blk_e, n_live, xb, w_gate_up, b_gate_up, w_down, b_down):
    cap, d = xb.shape
    n_blocks = cap // MOE_BLOCK
    grid_spec = pltpu.PrefetchScalarGridSpec(
        num_scalar_prefetch=2, grid=(n_blocks,),
        in_specs=[pl.BlockSpec((MOE_BLOCK, d), lambda i, be, nv: (i, 0)),
                  pl.BlockSpec((None, None, d, 2 * D_FF), lambda i, be, nv: (layer, be[i], 0, 0)),
                  pl.BlockSpec((None, None, 1, 2 * D_FF), lambda i, be, nv: (layer, be[i], 0, 0)),
                  pl.BlockSpec((None, None, D_FF, d), lambda i, be, nv: (layer, be[i], 0, 0)),
                  pl.BlockSpec((None, None, 1, d), lambda i, be, nv: (layer, be[i], 0, 0))],
        out_specs=pl.BlockSpec((MOE_BLOCK, d), lambda i, be, nv: (i, 0)),
        scratch_shapes=[pltpu.VMEM((d, 2 * D_FF), BF), pltpu.VMEM((D_FF, d), BF)])
    return pl.pallas_call(
        _experts_kernel,
        grid_spec=grid_spec,
        out_shape=jax.ShapeDtypeStruct((cap, d), F32),
        compiler_params=_params(1),
        name="experts",
    )(blk_e, n_live, xb, w_gate_up, b_gate_up.reshape(DEPTH, N_EXPERTS, 1, 2 * D_FF),
      w_down, b_down.reshape(DEPTH, N_EXPERTS, 1, d))


def _moe(layer, h, router_w_pad, router_b_pad, w_gate_up, b_gate_up, w_down, b_down):
    n, d = h.shape
    idx, gate = _router(h, router_w_pad, router_b_pad, tm=512)
    top_i = idx[:, :TOP_K]
    top_g = gate[:, :TOP_K]
    n_flat = n * TOP_K
    key_bits = (n_flat - 1).bit_length()
    flat_e = top_i.reshape(-1)
    flat_i = jnp.arange(n_flat, dtype=jnp.int32)
    sorted_keys = jnp.sort(flat_e * (1 << key_bits) + flat_i)
    st = (sorted_keys & ((1 << key_bits) - 1)) // TOP_K
    chunk = 256
    experts = jnp.arange(N_EXPERTS, dtype=jnp.int32)
    onehot = (flat_e[:, None] == experts[None, :]).reshape(n_flat // chunk, chunk, N_EXPERTS)
    within = jnp.einsum('ij,cjk->cik', jnp.tril(jnp.ones((chunk, chunk), BF)), onehot.astype(BF),
                        preferred_element_type=F32)
    chunk_tot = within[:, -1, :]
    chunk_off = jnp.cumsum(chunk_tot, axis=0) - chunk_tot
    counts = (chunk_off[-1] + chunk_tot[-1]).astype(jnp.int32)
    padded = (counts + MOE_BLOCK - 1) // MOE_BLOCK * MOE_BLOCK
    pad_end = jnp.cumsum(padded)
    pad_start = pad_end - padded
    start = jnp.cumsum(counts) - counts
    pos = jnp.sum(jnp.where(onehot, within + chunk_off[:, None, :] - 1.0
                            + pad_start.astype(F32)[None, None, :], 0.0),
                  axis=-1).astype(jnp.int32).reshape(n, TOP_K)
    n_blocks = n_flat // MOE_BLOCK + N_EXPERTS
    blk_row0 = jnp.arange(n_blocks, dtype=jnp.int32) * MOE_BLOCK
    blk_e = jnp.minimum(jnp.sum((pad_end[None, :] <= blk_row0[:, None]).astype(jnp.int32), axis=1),
                        N_EXPERTS - 1)
    blk_hot = blk_e[:, None] == experts[None, :]
    blk_off = blk_row0 - jnp.sum(jnp.where(blk_hot, pad_start[None, :], 0), axis=1)
    blk_cnt = jnp.sum(jnp.where(blk_hot, counts[None, :], 0), axis=1)
    blk_src = jnp.sum(jnp.where(blk_hot, start[None, :], 0), axis=1) + blk_off
    n_live = (pad_end[-1:] // MOE_BLOCK).astype(jnp.int32)
    lane_row = jnp.arange(MOE_BLOCK, dtype=jnp.int32)[None, :]
    row_ok = blk_off[:, None] + lane_row < blk_cnt[:, None]
    src = jnp.clip(blk_src[:, None] + lane_row, 0, n_flat - 1)
    buf_t = jnp.where(row_ok, st[src.reshape(-1)].reshape(n_blocks, MOE_BLOCK), 0).reshape(-1)
    xb = h[buf_t]
    yb = _experts(layer, blk_e, n_live, xb, w_gate_up, b_gate_up, w_down, b_down)
    y = yb[pos[:, 0]] * top_g[:, 0:1]
    for k in range(1, TOP_K):
        y = y + yb[pos[:, k]] * top_g[:, k:k + 1]
    return y


def kernel(x_prompt, x_sample, mem_prompt, cache_moba_k, cache_moba_v, cache_diff_k, cache_diff_v, cache_mem_k, cache_mem_v, state_conv, page_table, w_in, b_in, conv_w, conv_b, conv_ln_g, conv_ln_b, lam_q1, lam_k1, lam_q2, lam_k2, diff_norm_g, w_branch, w_out, ln1_g, ln1_b, xq_w, xk_w, xv_w, xo_w, ln2_g, ln2_b, router_w, router_b, w_gate_up, b_gate_up, w_down, b_down, ln3_g, ln3_b):
    bp, seq, d = x_prompt.shape
    db, dseq, _ = x_sample.shape
    n_p = bp * seq
    n_s = db * dseq
    n_pool = cache_moba_k.shape[1]
    hist_pad = 32

    o_qa, o_ka, o_va = 0, MIX_A, 2 * MIX_A
    o_u = 3 * MIX_A
    o_qc = o_u + 2 * C_B
    o_kc, o_vc = o_qc + MIX_C, o_qc + 2 * MIX_C
    o_g = o_qc + 3 * MIX_C

    def permute_cols(a):
        return jnp.concatenate([a[..., o_g:], a[..., o_qc:o_g], a[..., o_qa:o_u], a[..., o_u:o_qc]], axis=-1)

    pools = [jnp.transpose(c, (0, 1, 3, 4, 2)).reshape(DEPTH, n_pool, MIX_A, PAGE_SIZE)
             for c in (cache_moba_k, cache_moba_v)]
    pools += [c.reshape(DEPTH, n_pool, PAGE_SIZE * H_C, 2 * DH_C) for c in (cache_diff_k, cache_diff_v)]
    def mem_rows(c):
        c = c.reshape(DEPTH, db, MEM_LEN, XH, XDH // LANES, LANES)
        return jnp.transpose(c, (0, 1, 2, 4, 3, 5)).reshape(DEPTH, db, MEM_LEN * XH * XDH // LANES, LANES)

    mem_k_s = mem_rows(cache_mem_k)
    mem_v_s = mem_rows(cache_mem_v)
    mem2d = mem_prompt.reshape(bp * MEM_LEN, d)
    hist_p = jnp.zeros((bp, hist_pad, C_B), F32)
    hist_s = jnp.pad(state_conv, ((0, 0), (0, 0), (hist_pad - (CONV_W - 1), 0), (0, 0)))
    router_w_pad = jnp.pad(router_w, ((0, 0), (0, 0), (0, LANES - N_EXPERTS)))
    router_b_pad = jnp.pad(router_b, ((0, 0), (0, LANES - N_EXPERTS))).reshape(DEPTH, 1, LANES)

    x = jnp.concatenate([x_prompt.reshape(n_p, d), x_sample.reshape(n_s, d)], axis=0)
    outs = {k: [] for k in ("ka_p", "va_p", "kc_p", "vc_p", "mk_p", "mv_p", "cs_p",
                            "ka_s", "va_s", "kc_s", "vc_s", "cs_s")}

    for l in range(DEPTH):
        lam_init = 0.8 - 0.6 * math.exp(-0.3 * l)
        lam_vecs = [v[l].reshape(1, DH_C) for v in (lam_q1, lam_k1, lam_q2, lam_k2)]
        w_in_l = permute_cols(w_in[l]).astype(BF)
        b_in_l = permute_cols(b_in[l])
        wa = w_branch[l, :MIX_A].astype(BF)
        wb = w_branch[l, MIX_A:MIX_A + C_B].astype(BF)
        wc = w_branch[l, MIX_A + C_B:].astype(BF)

        proj = _matmul(x, w_in_l, b_in_l, tm=512, tn=IN_W // 2, name="proj_in")
        proj_s = proj[n_p:]
        new = {k: proj_s[:, c:c + w] for k, c, w in
               (("qa", COL_QA, MIX_A), ("ka", COL_KA, MIX_A), ("va", COL_VA, MIX_A),
                ("qc", COL_QC, MIX_C), ("kc", COL_KC, MIX_C), ("vc", COL_VC, MIX_C))}

        n_all = n_p + n_s
        oa = _moba_prompt(proj, bp, seq, n_all)
        oc = _diff_prompt(proj, lam_vecs, diff_norm_g[l], bp, seq, lam_init, n_all)
        oa, oc = _sample_attn(l, page_table, new["qa"], new["ka"], new["va"],
                              new["qc"], new["kc"], new["vc"], pools, lam_vecs,
                              diff_norm_g[l], lam_init, oa, oc, n_p)
        ob, cs_p = _conv_module(proj, 0, bp, seq, 1, hist_p, conv_w[l], conv_b[l],
                                conv_ln_g[l], conv_ln_b[l], "conv_prompt")
        ob, cs_s = _conv_module(proj, n_p, db, dseq, 16, hist_s[l], conv_w[l], conv_b[l],
                                conv_ln_g[l], conv_ln_b[l], "conv_sample", into=ob)
        h1 = _merge(oa, ob, oc, proj, x, wa, wb, wc, w_out[l].astype(BF), ln1_g[l], ln1_b[l], tm=256)

        mk_p = _matmul(mem2d, xk_w[l].astype(BF), tm=512, tn=d, name="mem_k")
        mv_p = _matmul(mem2d, xv_w[l].astype(BF), tm=512, tn=d, name="mem_v")
        q = _matmul(h1, xq_w[l].astype(BF), tm=1024, tn=d, name="xattn_q")
        xo = _xattn(q, 0, n_p, mk_p.reshape(bp, MEM_LEN, d), mv_p.reshape(bp, MEM_LEN, d),
                    l, 1, 512, seq // 512, "xattn_prompt")
        xo = _xattn(q, n_p, n_s, mem_k_s, mem_v_s, l, 4, dseq, 1, "xattn_sample", into=xo)
        h2 = _mm_res_ln(xo, xo_w[l].astype(BF), h1, ln2_g[l], ln2_b[l], tm=512, name="xattn_out")

        y = _moe(l, h2, router_w_pad[l], router_b_pad[l], w_gate_up, b_gate_up, w_down, b_down)
        x = _res_ln(y, h2, ln3_g[l], ln3_b[l], tm=512, name="moe_out")

        proj_p = proj[:n_p]
        outs["ka_p"].append(proj_p[:, COL_KA:COL_KA + MIX_A].reshape(bp, seq, H_A, DH_A))
        outs["va_p"].append(proj_p[:, COL_VA:COL_VA + MIX_A].reshape(bp, seq, H_A, DH_A))
        outs["kc_p"].append(proj_p[:, COL_KC:COL_KC + MIX_C].reshape(bp, seq, H_C, 2 * DH_C))
        outs["vc_p"].append(proj_p[:, COL_VC:COL_VC + MIX_C].reshape(bp, seq, H_C, 2 * DH_C))
        outs["mk_p"].append(mk_p.reshape(bp, MEM_LEN, XH, XDH))
        outs["mv_p"].append(mv_p.reshape(bp, MEM_LEN, XH, XDH))
        outs["cs_p"].append(cs_p)
        outs["ka_s"].append(new["ka"].reshape(db, dseq, H_A, DH_A))
        outs["va_s"].append(new["va"].reshape(db, dseq, H_A, DH_A))
        outs["kc_s"].append(new["kc"].reshape(db, dseq, H_C, 2 * DH_C))
        outs["vc_s"].append(new["vc"].reshape(db, dseq, H_C, 2 * DH_C))
        outs["cs_s"].append(cs_s)

    return (x[:n_p].reshape(bp, seq, d), x[n_p:].reshape(db, dseq, d),
            jnp.stack(outs["ka_p"]), jnp.stack(outs["va_p"]), jnp.stack(outs["kc_p"]),
            jnp.stack(outs["vc_p"]), jnp.stack(outs["mk_p"]), jnp.stack(outs["mv_p"]),
            jnp.stack(outs["cs_p"]), jnp.stack(outs["ka_s"]), jnp.stack(outs["va_s"]),
            jnp.stack(outs["kc_s"]), jnp.stack(outs["vc_s"]), jnp.stack(outs["cs_s"]))
```

```python
import functools
import math

import jax
import jax.numpy as jnp
from jax import lax
from jax.experimental import pallas as pl
from jax.experimental.pallas import tpu as pltpu

D_MODEL = 1024
DEPTH = 2
PAST_LEN = 2048
PAGE_SIZE = 128
N_PAGES = PAST_LEN // PAGE_SIZE

H_A = 4
DH_A = 64
MOBA_BLOCK = 256
MOBA_TOPK = 3
C_B = 256
CONV_W = 31
H_C = 4
DH_C = 64
MEM_LEN = 256
XH = 4
XDH = D_MODEL // XH
N_EXPERTS = 32
TOP_K = 4
D_FF = D_MODEL
SWIGLU_LIMIT = 7.0
SWIGLU_ALPHA = 1.702
MIX_A = H_A * DH_A
MIX_C = H_C * 2 * DH_C
N_BRANCH = 3
DN_ALPHA = (2 * DEPTH) ** 0.25
LN_EPS = 1e-5
LOG2E = math.log2(math.e)

SLOPES_C = tuple(2.0 ** -(i + 1) for i in range(0, 4))
SLOPES_A = tuple(2.0 ** -(i + 1) for i in range(4, 8))

COL_G = 0
COL_QC = 3 * D_MODEL
COL_KC = COL_QC + MIX_C
COL_VC = COL_KC + MIX_C
COL_QA = COL_VC + MIX_C
COL_KA = COL_QA + MIX_A
COL_VA = COL_KA + MIX_A
COL_UA = COL_VA + MIX_A
COL_UB = COL_UA + C_B
IN_W = COL_UB + C_B

LANES = 128
SUBLANES = 8
Q_TILE = MOBA_BLOCK
KEY_TILE = MOBA_BLOCK
MOE_BLOCK = 256
VMEM_LIMIT = 56 * 1024 * 1024

_NT = (((1,), (1,)), ((), ()))
_NEG = -jnp.inf
BF = jnp.bfloat16
F32 = jnp.float32


def _params(n_grid):
    return pltpu.CompilerParams(dimension_semantics=("arbitrary",) * n_grid,
                                vmem_limit_bytes=VMEM_LIMIT)


def _sigmoid(x):
    return 1.0 / (1.0 + jnp.exp(-x))


def _layernorm(x, g, b):
    mu = jnp.mean(x, axis=-1, keepdims=True)
    xc = x - mu
    var = jnp.mean(xc * xc, axis=-1, keepdims=True)
    return xc * lax.rsqrt(var + LN_EPS) * g + b


def _top_picks(g, pos, k, axis=-1):
    lane = pos.astype(F32)
    sel = jnp.zeros(g.shape, F32)
    vals, idxs = [], []
    for _ in range(k):
        m = jnp.max(g, axis=axis, keepdims=True)
        cand = jnp.where(g == m, lane, float(g.shape[axis]))
        idx = jnp.where(m > _NEG, jnp.min(cand, axis=axis, keepdims=True), -1.0)
        pick = lane == idx
        sel = jnp.where(pick, 1.0, sel)
        g = jnp.where(pick, _NEG, g)
        vals.append(m)
        idxs.append(idx)
    return vals, idxs, sel


def _mm_kernel(x_ref, w_ref, *rest, has_bias):
    o_ref = rest[-1]
    acc = jnp.dot(x_ref[...].astype(BF), w_ref[...], preferred_element_type=F32)
    if has_bias:
        acc = acc + rest[0][...]
    o_ref[...] = acc


def _matmul(x, w, b=None, *, tm, tn, name):
    m, k = x.shape
    n = w.shape[1]
    in_specs = [pl.BlockSpec((tm, k), lambda j, i: (i, 0)),
                pl.BlockSpec((k, tn), lambda j, i: (0, j))]
    args = [x, w]
    if b is not None:
        in_specs.append(pl.BlockSpec((1, tn), lambda j, i: (0, j)))
        args.append(b.reshape(1, n))
    return pl.pallas_call(
        functools.partial(_mm_kernel, has_bias=b is not None),
        grid=(n // tn, pl.cdiv(m, tm)),
        in_specs=in_specs,
        out_specs=pl.BlockSpec((tm, tn), lambda j, i: (i, j)),
        out_shape=jax.ShapeDtypeStruct((m, n), F32),
        compiler_params=_params(2),
        name=name,
    )(*args)


def _mm_res_ln_kernel(x_ref, w_ref, r_ref, g_ref, b_ref, o_ref):
    y = jnp.dot(x_ref[...].astype(BF), w_ref[...], preferred_element_type=F32)
    o_ref[...] = _layernorm(DN_ALPHA * r_ref[...] + y, g_ref[...], b_ref[...])


def _mm_res_ln(x, w, res, g, b, *, tm, name):
    m, k = x.shape
    n = w.shape[1]
    return pl.pallas_call(
        _mm_res_ln_kernel,
        grid=(pl.cdiv(m, tm),),
        in_specs=[pl.BlockSpec((tm, k), lambda i: (i, 0)),
                  pl.BlockSpec((k, n), lambda i: (0, 0)),
                  pl.BlockSpec((tm, n), lambda i: (i, 0)),
                  pl.BlockSpec((1, n), lambda i: (0, 0)),
                  pl.BlockSpec((1, n), lambda i: (0, 0))],
        out_specs=pl.BlockSpec((tm, n), lambda i: (i, 0)),
        out_shape=jax.ShapeDtypeStruct((m, n), F32),
        compiler_params=_params(1),
        name=name,
    )(x, w, res, g.reshape(1, n), b.reshape(1, n))


def _combine_ln_kernel(*refs):
    ys, (w_ref, r_ref, g_ref, b_ref, o_ref) = refs[:TOP_K], refs[TOP_K:]
    y = ys[0][...] * w_ref[:, 0:1]
    for k in range(1, TOP_K):
        y = y + ys[k][...] * w_ref[:, k:k + 1]
    o_ref[...] = _layernorm(DN_ALPHA * r_ref[...] + y, g_ref[...], b_ref[...])


def _combine_ln(ys, weights, res, g, b, *, tm, name):
    m, n = res.shape
    row_spec = pl.BlockSpec((tm, n), lambda i: (i, 0))
    return pl.pallas_call(
        _combine_ln_kernel,
        grid=(pl.cdiv(m, tm),),
        in_specs=[row_spec] * TOP_K + [pl.BlockSpec((tm, LANES), lambda i: (i, 0)), row_spec,
                                       pl.BlockSpec((1, n), lambda i: (0, 0)),
                                       pl.BlockSpec((1, n), lambda i: (0, 0))],
        out_specs=row_spec,
        out_shape=jax.ShapeDtypeStruct((m, n), F32),
        compiler_params=_params(1),
        name=name,
    )(*ys, weights, res, g.reshape(1, n), b.reshape(1, n))


def _flash_block(chains, carry):
    ss = []
    for k, q, bias, mask, _, _ in chains:
        s = lax.dot_general(k, q, _NT, preferred_element_type=F32) + bias
        ss.append(s if mask is None else jnp.where(mask, s, _NEG))
    stats = []
    for i, (_, _, _, _, shift, _) in enumerate(chains):
        top = jnp.max(ss[i], axis=0, keepdims=True)
        if carry is None:
            m_new, a = top, None
            p = jnp.exp2(ss[i] - m_new)
            l = jnp.sum(p, axis=0, keepdims=True)
        else:
            m, l_old, _ = carry[i]
            m_new = jnp.maximum(m, top - shift)
            a = jnp.exp2(m - m_new)
            p = jnp.exp2(ss[i] - (m_new + shift))
            l = a * l_old + jnp.sum(p, axis=0, keepdims=True)
        stats.append((m_new, l, a, p.astype(BF)))
    out = []
    for i, (_, _, _, _, _, v_t) in enumerate(chains):
        m_new, l, a, p = stats[i]
        pv = jnp.dot(v_t, p, preferred_element_type=F32)
        out.append((m_new, l, pv if carry is None else a * carry[i][2] + pv))
    return tuple(out)


def _tile_geometry():
    key_j = lax.broadcasted_iota(jnp.int32, (KEY_TILE, Q_TILE), 0)
    qry_i = lax.broadcasted_iota(jnp.int32, (KEY_TILE, Q_TILE), 1)
    causal = qry_i >= key_j
    rel = (key_j - (KEY_TILE - 1)).astype(F32)
    return causal, rel


def _stage_kv(k_ref, v_ref, kb_ref, vt_ref, n_blocks):
    kb_ref[...] = k_ref[...].astype(BF)
    for n in range(n_blocks):
        vt_ref[n] = v_ref[n * KEY_TILE:(n + 1) * KEY_TILE, :].T.astype(BF)


def _moba_prompt_kernel(q_ref, k_ref, v_ref, into_ref, o_ref, kb_ref, vt_ref, km_ref, *, n_blocks):
    del into_ref
    cur = pl.program_id(1)

    @pl.when(cur == 0)
    def _():
        _stage_kv(k_ref, v_ref, kb_ref, vt_ref, n_blocks)
        km_ref[...] = jnp.zeros(km_ref.shape, F32)
        for n in range(n_blocks):
            km_ref[n:n + 1, :] = jnp.mean(k_ref[n * MOBA_BLOCK:(n + 1) * MOBA_BLOCK, :],
                                          axis=0, keepdims=True)

    causal, rel = _tile_geometry()
    q = q_ref[...]
    lane_q = lax.broadcasted_iota(jnp.int32, q.shape, 1)
    lane_p = lax.broadcasted_iota(jnp.int32, (Q_TILE, LANES), 1)
    blk_row = lax.broadcasted_iota(jnp.int32, (km_ref.shape[0], Q_TILE), 0)

    consts = []
    for h in range(H_A):
        pair, hh = divmod(h, 2)
        cols = slice(pair * LANES, (pair + 1) * LANES)
        slope2 = SLOPES_A[h] * LOG2E
        gate = lax.dot_general(km_ref[...], jnp.where(lane_q // DH_A == h, q, 0.0), _NT,
                               precision=lax.Precision.HIGHEST,
                               preferred_element_type=F32)
        gate = jnp.where(blk_row < cur, gate, _NEG)
        _, _, sel = _top_picks(gate, blk_row, MOBA_TOPK, axis=0)
        qm = jnp.where(lane_p // DH_A == hh, q[:, cols] * (DH_A ** -0.5 * LOG2E), 0.0).astype(BF)
        consts.append((cols, slope2, sel, qm, slope2 * rel))

    def block(n, carry):
        start = pl.multiple_of(n * KEY_TILE, KEY_TILE)
        chains = []
        for cols, slope2, sel, qm, bias in consts:
            if carry is None:
                mask, shift = causal, None
            else:
                mask = jnp.max(jnp.where(blk_row == n, sel, 0.0), axis=0, keepdims=True) > 0.5
                shift = slope2 * jnp.full((1, Q_TILE), (cur - n) * KEY_TILE, jnp.int32).astype(F32)
            chains.append((kb_ref[pl.ds(start, KEY_TILE), cols], qm, bias, mask, shift,
                           vt_ref[n, cols, :]))
        return _flash_block(chains, carry)

    final = lax.fori_loop(0, cur, block, block(cur, None))
    row = lax.broadcasted_iota(jnp.int32, (LANES, Q_TILE), 0)
    for pair in range(H_A // 2):
        (_, l0, a0), (_, l1, a1) = final[2 * pair], final[2 * pair + 1]
        o_t = jnp.where(row < DH_A, a0 / l0, a1 / l1)
        o_ref[:, pair * LANES:(pair + 1) * LANES] = o_t.T


def _moba_prompt(proj, batch, seq, into):
    n_q = seq // Q_TILE
    n_blocks = seq // MOBA_BLOCK
    km_rows = -(-n_blocks // SUBLANES) * SUBLANES
    cq, ck, cv = COL_QA // MIX_A, COL_KA // MIX_A, COL_VA // MIX_A
    return pl.pallas_call(
        functools.partial(_moba_prompt_kernel, n_blocks=n_blocks),
        grid=(batch, n_q),
        in_specs=[pl.BlockSpec((Q_TILE, MIX_A), lambda b, i: (b * n_q + i, cq)),
                  pl.BlockSpec((seq, MIX_A), lambda b, i: (b, ck)),
                  pl.BlockSpec((seq, MIX_A), lambda b, i: (b, cv)),
                  pl.BlockSpec(memory_space=pl.ANY)],
        out_specs=pl.BlockSpec((Q_TILE, MIX_A), lambda b, i: (b * n_q + i, 0)),
        out_shape=jax.ShapeDtypeStruct(into.shape, F32),
        input_output_aliases={3: 0},
        scratch_shapes=[pltpu.VMEM((seq, MIX_A), BF),
                        pltpu.VMEM((n_blocks, MIX_A, KEY_TILE), BF),
                        pltpu.VMEM((km_rows, MIX_A), F32)],
        compiler_params=_params(2),
        name="moba_prompt",
    )(proj, proj, proj, into)


def _lambda(lq1, lk1, lq2, lk2, lam_init):
    return (jnp.exp(jnp.sum(lq1[...] * lk1[...], axis=-1, keepdims=True))
            - jnp.exp(jnp.sum(lq2[...] * lk2[...], axis=-1, keepdims=True)) + lam_init)


def _diff_prompt_kernel(q_ref, k_ref, v_ref, lq1, lk1, lq2, lk2, ng_ref, into_ref, o_ref, kb_ref,
                        vt_ref, *, lam_init, n_blocks):
    del into_ref
    cur = pl.program_id(1)

    @pl.when(cur == 0)
    def _():
        _stage_kv(k_ref, v_ref, kb_ref, vt_ref, n_blocks)

    causal, rel = _tile_geometry()
    lam = _lambda(lq1, lk1, lq2, lk2, lam_init)
    lane = lax.broadcasted_iota(jnp.int32, (Q_TILE, LANES), 1)

    consts = []
    for h in range(H_C):
        cols = slice(h * LANES, (h + 1) * LANES)
        slope2 = SLOPES_C[h] * LOG2E
        qh = q_ref[:, cols] * (DH_C ** -0.5 * LOG2E)
        consts.append((cols, slope2, jnp.where(lane < DH_C, qh, 0.0).astype(BF),
                       jnp.where(lane >= DH_C, qh, 0.0).astype(BF), slope2 * rel))

    def block(n, carry):
        start = pl.multiple_of(n * KEY_TILE, KEY_TILE)
        chains = []
        for cols, slope2, q1, q2, bias in consts:
            kb = kb_ref[pl.ds(start, KEY_TILE), cols]
            v_t = vt_ref[n, cols, :]
            if carry is None:
                mask, shift = causal, None
            else:
                mask = None
                shift = slope2 * jnp.full((1, Q_TILE), (cur - n) * KEY_TILE, jnp.int32).astype(F32)
            chains += [(kb, q1, bias, mask, shift, v_t), (kb, q2, bias, mask, shift, v_t)]
        return _flash_block(chains, carry)

    final = lax.fori_loop(0, cur, block, block(cur, None))
    for h in range(H_C):
        (_, l1, a1), (_, l2, a2) = final[2 * h], final[2 * h + 1]
        o_t = a1 / l1 - lam * (a2 / l2)
        o_t = o_t * lax.rsqrt(jnp.mean(o_t * o_t, axis=0, keepdims=True) + LN_EPS) * ng_ref[...]
        o_ref[:, h * LANES:(h + 1) * LANES] = (o_t * (1.0 - lam_init)).T


def _diff_prompt(proj, lam_vecs, norm_g, batch, seq, lam_init, into):
    n_q = seq // Q_TILE
    n_blocks = seq // KEY_TILE
    cq, ck, cv = COL_QC // MIX_C, COL_KC // MIX_C, COL_VC // MIX_C
    vec = pl.BlockSpec((1, DH_C), lambda b, i: (0, 0))
    return pl.pallas_call(
        functools.partial(_diff_prompt_kernel, lam_init=lam_init, n_blocks=n_blocks),
        grid=(batch, n_q),
        in_specs=[pl.BlockSpec((Q_TILE, MIX_C), lambda b, i: (b * n_q + i, cq)),
                  pl.BlockSpec((seq, MIX_C), lambda b, i: (b, ck)),
                  pl.BlockSpec((seq, MIX_C), lambda b, i: (b, cv)),
                  vec, vec, vec, vec,
                  pl.BlockSpec((2 * DH_C, 1), lambda b, i: (0, 0)),
                  pl.BlockSpec(memory_space=pl.ANY)],
        out_specs=pl.BlockSpec((Q_TILE, MIX_C), lambda b, i: (b * n_q + i, 0)),
        out_shape=jax.ShapeDtypeStruct(into.shape, F32),
        input_output_aliases={8: 0},
        scratch_shapes=[pltpu.VMEM((seq, MIX_C), BF),
                        pltpu.VMEM((n_blocks, MIX_C, KEY_TILE), BF)],
        compiler_params=_params(2),
        name="diff_prompt",
    )(proj, proj, proj, *lam_vecs, norm_g.reshape(2 * DH_C, 1), into)


def _sample_attn_kernel(pt_ref, qa_ref, ka_ref, va_ref, qc_ref, kc_ref, vc_ref,
                        lq1, lk1, lq2, lk2, ng_ref, mk_hbm, mv_hbm, dk_hbm, dv_hbm, oa_all, oc_all,
                        oa_ref, oc_ref, mk_buf, mv_buf, dk_buf, dv_buf, sems, *, layer, lam_init, t_new):
    del oa_all, oc_all
    n = N_PAGES
    seq = pl.program_id(0)
    slot = seq % 2
    streams = ((mk_hbm, mk_buf), (mv_hbm, mv_buf), (dk_hbm, dk_buf), (dv_hbm, dv_buf))

    def page_copies(s, to_slot):
        copies = []
        for j in range(n):
            page = pt_ref[s * n + j]
            for p, (pool, buf) in enumerate(streams):
                copies.append(pltpu.make_async_copy(pool.at[layer, page], buf.at[to_slot, j],
                                                    sems.at[to_slot, p]))
        return copies

    @pl.when(seq == 0)
    def _():
        for c in page_copies(seq, slot):
            c.start()

    @pl.when(seq + 1 < pl.num_programs(0))
    def _():
        for c in page_copies(seq + 1, 1 - slot):
            c.start()

    for c in page_copies(seq, slot):
        c.wait()

    mk = [mk_buf.at[slot, j] for j in range(n)]
    mv = [mv_buf.at[slot, j] for j in range(n)]
    dk = [dk_buf.at[slot, j] for j in range(n)]
    dv = [dv_buf.at[slot, j] for j in range(n)]
    pad_rows = PAGE_SIZE - t_new

    def pad_new(x):
        return jnp.concatenate([x, jnp.zeros((pad_rows, x.shape[1]), x.dtype)], axis=0)

    def head_rows(ref):
        return jnp.concatenate([ref[pl.ds(h, PAGE_SIZE, stride=H_C), :] for h in range(H_C)],
                               axis=1)

    def raw_scores(q_rows, page_scores, k_new):
        qb = q_rows.astype(BF)
        tiles = [page_scores(qb, j) for j in range(n)]
        tiles.append(lax.dot_general(qb, pad_new(k_new).astype(BF), _NT, preferred_element_type=F32))
        return tiles

    def softmax_tiles(tiles, slope_col, t_col, gate_sel):
        shape = tiles[0].shape
        lane_f = lax.broadcasted_iota(jnp.int32, shape, 1).astype(F32)
        slope = jnp.broadcast_to(slope_col, shape)
        t_row = jnp.broadcast_to(t_col, shape)
        near = slope * (lane_f - t_row)
        out = []
        for j in range(n):
            s = tiles[j] + (near - slope * float(PAST_LEN - j * PAGE_SIZE))
            if gate_sel is not None:
                s = jnp.where(gate_sel[j // (MOBA_BLOCK // PAGE_SIZE)], s, _NEG)
            out.append(s)
        out.append(jnp.where(lane_f <= t_row, tiles[n] + near, _NEG))
        top = out[0]
        for s in out[1:]:
            top = jnp.maximum(top, s)
        m = jnp.max(top, axis=-1, keepdims=True)
        ps = [jnp.exp(s - m) for s in out]
        tot = ps[0]
        for p in ps[1:]:
            tot = tot + p
        return ps, jnp.sum(tot, axis=-1, keepdims=True)

    def fold_heads(o, head_w, n_heads):
        r = lax.broadcasted_iota(jnp.int32, o.shape, 0)
        c = lax.broadcasted_iota(jnp.int32, o.shape, 1)
        o = jnp.where(r // t_new == c // head_w, o, 0.0)
        out = o[0:t_new]
        for h in range(1, n_heads):
            out = out + o[h * t_new:(h + 1) * t_new]
        return out

    def row_const(rows, values):
        h = lax.broadcasted_iota(jnp.int32, (rows, 1), 0) // t_new
        out = jnp.full((rows, 1), values[-1], F32)
        for i in range(len(values) - 2, -1, -1):
            out = jnp.where(h % len(values) == i, values[i], out)
        return out

    ra = H_A * t_new
    qa = qa_ref[...] * (DH_A ** -0.5)
    qa_rows = jnp.concatenate([qa] * H_A, axis=0)
    r = lax.broadcasted_iota(jnp.int32, qa_rows.shape, 0)
    c = lax.broadcasted_iota(jnp.int32, qa_rows.shape, 1)
    qa_rows = jnp.where(r // t_new == c // DH_A, qa_rows, 0.0)
    row_t = (lax.broadcasted_iota(jnp.int32, (ra, 1), 0) % t_new).astype(F32)
    row_slope = row_const(ra, SLOPES_A)
    pages_per_block = MOBA_BLOCK // PAGE_SIZE
    n_past_blocks = PAST_LEN // MOBA_BLOCK

    rc = H_C * t_new
    qc = qc_ref[...] * (DH_C ** -0.5)
    qc_rows = jnp.concatenate([qc] * (2 * H_C), axis=0)
    r = lax.broadcasted_iota(jnp.int32, qc_rows.shape, 0)
    c = lax.broadcasted_iota(jnp.int32, qc_rows.shape, 1)
    keep = ((r % rc) // t_new == c // (2 * DH_C)) & (r // rc == (c % (2 * DH_C)) // DH_C)
    qc_rows = jnp.where(keep, qc_rows, 0.0)
    row_t_c = (lax.broadcasted_iota(jnp.int32, (2 * rc, 1), 0) % t_new).astype(F32)
    row_slope_c = row_const(2 * rc, SLOPES_C)
    lam = _lambda(lq1, lk1, lq2, lk2, lam_init)

    raw_a = raw_scores(qa_rows,
                       lambda qb, j: jnp.dot(qb, mk[j][...].astype(BF), preferred_element_type=F32),
                       ka_ref[...])
    raw_c = raw_scores(qc_rows,
                       lambda qb, j: lax.dot_general(qb, head_rows(dk[j]).astype(BF), _NT,
                                                     preferred_element_type=F32),
                       kc_ref[...])
    lane_g = lax.broadcasted_iota(jnp.int32, (ra, PAGE_SIZE), 1)
    gate = jnp.full((ra, PAGE_SIZE), _NEG, F32)
    for b in range(n_past_blocks):
        tot = raw_a[b * pages_per_block]
        for j in range(1, pages_per_block):
            tot = tot + raw_a[b * pages_per_block + j]
        gate = jnp.where(lane_g == b, jnp.sum(tot, axis=-1, keepdims=True), gate)
    _, _, sel = _top_picks(gate, lane_g, MOBA_TOPK)
    gate_sel = [jnp.broadcast_to(jnp.max(jnp.where(lane_g == b, sel, 0.0), axis=-1, keepdims=True),
                                 (ra, PAGE_SIZE)) > 0.5 for b in range(n_past_blocks)]
    ps_a, l_a = softmax_tiles(raw_a, row_slope, row_t, gate_sel)
    ps_c, l_c = softmax_tiles(raw_c, row_slope_c, row_t_c, None)
    inv1 = jnp.broadcast_to(1.0 / l_c[0:rc], (rc, PAGE_SIZE))
    inv2 = jnp.broadcast_to(lam / l_c[rc:2 * rc], (rc, PAGE_SIZE))
    ws = [p[0:rc] * inv1 - p[rc:2 * rc] * inv2 for p in ps_c]
    oa = jnp.dot(ps_a[n].astype(BF), pad_new(va_ref[...]).astype(BF), preferred_element_type=F32)
    for j in range(n):
        oa = oa + lax.dot_general(ps_a[j].astype(BF), mv[j][...].astype(BF), _NT,
                                  preferred_element_type=F32)
    oc = jnp.dot(ws[n].astype(BF), pad_new(vc_ref[...]).astype(BF), preferred_element_type=F32)
    for j in range(n):
        oc = oc + jnp.dot(ws[j].astype(BF), head_rows(dv[j]).astype(BF),
                          preferred_element_type=F32)
    oa_ref[...] = fold_heads(oa / l_a, DH_A, H_A)
    oc = fold_heads(oc, 2 * DH_C, H_C)
    for h in range(H_C):
        cols = slice(h * LANES, (h + 1) * LANES)
        o = oc[:, cols]
        o = o * lax.rsqrt(jnp.mean(o * o, axis=-1, keepdims=True) + LN_EPS) * ng_ref[...]
        oc_ref[:, cols] = o * (1.0 - lam_init)


def _sample_attn(layer, page_table, qa, ka, va, qc, kc, vc, pools, lam_vecs, norm_g, lam_init,
                 oa_all, oc_all, row0):
    n_seq = page_table.shape[0]
    t_new = qa.shape[0] // n_seq
    pt_flat = page_table.reshape(-1)

    def new_spec(w):
        return pl.BlockSpec((t_new, w), lambda b, pt: (b, 0))

    vec = pl.BlockSpec((1, DH_C), lambda b, pt: (0, 0))
    in_specs = [new_spec(MIX_A)] * 3 + [new_spec(MIX_C)] * 3 + [vec] * 4
    in_specs.append(pl.BlockSpec((1, 2 * DH_C), lambda b, pt: (0, 0)))
    args = [qa, ka, va, qc, kc, vc, *lam_vecs, norm_g.reshape(1, 2 * DH_C)]
    in_specs += [pl.BlockSpec(memory_space=pl.ANY)] * len(pools)
    args += list(pools)
    blk0 = row0 // t_new
    first_alias = 1 + len(args)
    in_specs += [pl.BlockSpec(memory_space=pl.ANY)] * 2
    args += [oa_all, oc_all]
    n_slots = 2
    grid_spec = pltpu.PrefetchScalarGridSpec(
        num_scalar_prefetch=1, grid=(n_seq,), in_specs=in_specs,
        out_specs=[pl.BlockSpec((t_new, MIX_A), lambda b, pt: (blk0 + b, 0)),
                   pl.BlockSpec((t_new, MIX_C), lambda b, pt: (blk0 + b, 0))],
        scratch_shapes=[pltpu.VMEM((n_slots, N_PAGES) + pool.shape[2:], F32) for pool in pools]
        + [pltpu.SemaphoreType.DMA((n_slots, len(pools)))])
    return pl.pallas_call(
        functools.partial(_sample_attn_kernel, layer=layer, lam_init=lam_init, t_new=t_new),
        grid_spec=grid_spec,
        out_shape=[jax.ShapeDtypeStruct(oa_all.shape, F32), jax.ShapeDtypeStruct(oc_all.shape, F32)],
        input_output_aliases={first_alias: 0, first_alias + 1: 1},
        compiler_params=_params(1),
        name="sample_attn",
    )(pt_flat, *args)


def _conv_kernel(ua_ref, ub_ref, hist_ref, cw_ref, cb_ref, g_ref, b_ref, into_ref, o_ref, st_ref, z_ref,
                 *, n_seq, t_len, chunk):
    del into_ref
    pad = hist_ref.shape[1]
    lead = pad - (CONV_W - 1)
    for s in range(n_seq):
        rows = slice(s * t_len, (s + 1) * t_len)
        z_ref[0:pad, :] = hist_ref[s]
        z_ref[pad:pad + t_len, :] = ua_ref[rows, :] * _sigmoid(ub_ref[rows, :])
        for c0 in range(0, t_len, chunk):
            acc = jnp.zeros((chunk, C_B), F32) + cb_ref[...]
            for w in range(CONV_W):
                lo = c0 + lead + w
                acc = acc + z_ref[lo:lo + chunk, :] * cw_ref[w:w + 1, :]
            y = _layernorm(acc, g_ref[...], b_ref[...])
            o_ref[s * t_len + c0:s * t_len + c0 + chunk, :] = y * _sigmoid(y)
        st_ref[s] = z_ref[lead + t_len:pad + t_len, :]


def _conv_module(proj, row0, n_seq_total, t_len, seq_per_step, hist, conv_w, conv_b, ln_g, ln_b,
                 name, into):
    rows = seq_per_step * t_len
    blk0 = row0 // rows
    pad = hist.shape[1]
    vec = pl.BlockSpec((1, C_B), lambda i: (0, 0))
    return pl.pallas_call(
        functools.partial(_conv_kernel, n_seq=seq_per_step, t_len=t_len, chunk=min(t_len, 256)),
        grid=(n_seq_total // seq_per_step,),
        in_specs=[pl.BlockSpec((rows, C_B), lambda i: (blk0 + i, COL_UA // C_B)),
                  pl.BlockSpec((rows, C_B), lambda i: (blk0 + i, COL_UB // C_B)),
                  pl.BlockSpec((seq_per_step, pad, C_B), lambda i: (i, 0, 0)),
                  pl.BlockSpec((CONV_W, C_B), lambda i: (0, 0)),
                  vec, vec, vec, pl.BlockSpec(memory_space=pl.ANY)],
        out_specs=[pl.BlockSpec((rows, C_B), lambda i: (blk0 + i, 0)),
                   pl.BlockSpec((seq_per_step, CONV_W - 1, C_B), lambda i: (i, 0, 0))],
        out_shape=[jax.ShapeDtypeStruct(into.shape, F32),
                   jax.ShapeDtypeStruct((n_seq_total, CONV_W - 1, C_B), F32)],
        input_output_aliases={7: 0},
        scratch_shapes=[pltpu.VMEM((pad + t_len, C_B), F32)],
        compiler_params=_params(1),
        name=name,
    )(proj, proj, hist, conv_w, conv_b.reshape(1, C_B), ln_g.reshape(1, C_B), ln_b.reshape(1, C_B),
      into)


def _merge_kernel(oa_ref, ob_ref, oc_ref, g0_ref, g1_ref, g2_ref, x_ref, wa_ref, wb_ref, wc_ref,
                  wo_ref, g_ref, b_ref, o_ref):
    def branch(o_r, w_r, gate_r):
        y = jnp.dot(o_r[...].astype(BF), w_r[...], preferred_element_type=F32)
        return _sigmoid(gate_r[...]) * y

    merged = branch(oa_ref, wa_ref, g0_ref) + branch(ob_ref, wb_ref, g1_ref) + branch(oc_ref, wc_ref, g2_ref)
    y = jnp.dot(merged.astype(BF), wo_ref[...], preferred_element_type=F32)
    o_ref[...] = _layernorm(DN_ALPHA * x_ref[...] + y, g_ref[...], b_ref[...])


def _merge(oa, ob, oc, proj, x, wa, wb, wc, wo, g, b, *, tm):
    m = x.shape[0]
    d = D_MODEL

    def rows(w, col=0):
        return pl.BlockSpec((tm, w), lambda i, col=col: (i, col))

    def whole(a):
        return pl.BlockSpec(a.shape, lambda i: (0, 0))

    vec = pl.BlockSpec((1, d), lambda i: (0, 0))
    return pl.pallas_call(
        _merge_kernel,
        grid=(pl.cdiv(m, tm),),
        in_specs=[rows(MIX_A), rows(C_B), rows(MIX_C), rows(d, 0), rows(d, 1), rows(d, 2), rows(d),
                  whole(wa), whole(wb), whole(wc), whole(wo), vec, vec],
        out_specs=rows(d),
        out_shape=jax.ShapeDtypeStruct((m, d), F32),
        compiler_params=_params(1),
        name="merge",
    )(oa, ob, oc, proj, proj, proj, x, wa, wb, wc, wo, g.reshape(1, d), b.reshape(1, d))


def _xattn_kernel(q_ref, mk_ref, mv_ref, into_ref, o_ref, *, n_seq, t_len, interleaved):
    del into_ref
    def head(ref, s, h):
        if interleaved:
            tiles = XDH // LANES
            return jnp.concatenate(
                [ref[s, pl.ds(c * XH + h, MEM_LEN, stride=XH * tiles), :] for c in range(tiles)],
                axis=1).astype(BF)
        return ref[s, :, h * XDH:(h + 1) * XDH].astype(BF)

    for s in range(n_seq):
        rows = slice(s * t_len, (s + 1) * t_len)
        for h in range(XH):
            cols = slice(h * XDH, (h + 1) * XDH)
            qh = (q_ref[rows, cols] * (XDH ** -0.5)).astype(BF)
            sc = lax.dot_general(qh, head(mk_ref, s, h), _NT, preferred_element_type=F32)
            m = jnp.max(sc, axis=-1, keepdims=True)
            p = jnp.exp(sc - m)
            l = jnp.sum(p, axis=-1, keepdims=True)
            o = jnp.dot(p.astype(BF), head(mv_ref, s, h), preferred_element_type=F32)
            o_ref[rows, cols] = o / l


def _xattn(q, row0, n_rows, mem_k, mem_v, layer, seq_per_step, t_len, steps_per_mem, name, into):
    rows = seq_per_step * t_len
    blk0 = row0 // rows
    interleaved = mem_k.ndim == 4
    if interleaved:
        mem_spec = pl.BlockSpec((None, seq_per_step) + mem_k.shape[2:],
                                lambda i: (layer, i // steps_per_mem, 0, 0))
    else:
        mem_spec = pl.BlockSpec((seq_per_step, MEM_LEN, D_MODEL), lambda i: (i // steps_per_mem, 0, 0))
    return pl.pallas_call(
        functools.partial(_xattn_kernel, n_seq=seq_per_step, t_len=t_len, interleaved=interleaved),
        grid=(n_rows // rows,),
        in_specs=[pl.BlockSpec((rows, D_MODEL), lambda i: (blk0 + i, 0)), mem_spec, mem_spec,
                  pl.BlockSpec(memory_space=pl.ANY)],
        out_specs=pl.BlockSpec((rows, D_MODEL), lambda i: (blk0 + i, 0)),
        out_shape=jax.ShapeDtypeStruct(into.shape, F32),
        input_output_aliases={3: 0},
        compiler_params=_params(1),
        name=name,
    )(q, mem_k, mem_v, into)


def _router_kernel(x_ref, w_ref, b_ref, idx_ref, gate_ref):
    logits = jnp.dot(x_ref[...], w_ref[...], precision=lax.Precision.HIGHEST,
                     preferred_element_type=F32) + b_ref[...]
    lane = lax.broadcasted_iota(jnp.int32, logits.shape, 1)
    logits = jnp.where(lane < N_EXPERTS, logits, _NEG)
    vals, idxs, _ = _top_picks(logits, lane, TOP_K)
    es = [jnp.exp(v - vals[0]) for v in vals]
    denom = es[0]
    for e in es[1:]:
        denom = denom + e
    idx_out = jnp.zeros(logits.shape, jnp.int32)
    gate_out = jnp.zeros(logits.shape, F32)
    for k in range(TOP_K):
        idx_out = jnp.where(lane == k, idxs[k].astype(jnp.int32), idx_out)
        gate_out = jnp.where(lane == k, es[k] / denom, gate_out)
    idx_ref[...] = idx_out
    gate_ref[...] = gate_out


def _router(x, w_pad, b_pad, *, tm):
    m, d = x.shape
    return pl.pallas_call(
        _router_kernel,
        grid=(pl.cdiv(m, tm),),
        in_specs=[pl.BlockSpec((tm, d), lambda i: (i, 0)),
                  pl.BlockSpec((d, LANES), lambda i: (0, 0)),
                  pl.BlockSpec((1, LANES), lambda i: (0, 0))],
        out_specs=[pl.BlockSpec((tm, LANES), lambda i: (i, 0))] * 2,
        out_shape=[jax.ShapeDtypeStruct((m, LANES), jnp.int32),
                   jax.ShapeDtypeStruct((m, LANES), F32)],
        compiler_params=_params(1),
        name="router",
    )(x, w_pad, b_pad)


def _experts_kernel(be_ref, nv_ref, x_ref, wgu_ref, bgu_ref, wd_ref, bd_ref, o_ref, wgu_b, wd_b):
    i = pl.program_id(0)
    e = be_ref[i]
    prev = be_ref[jnp.maximum(i - 1, 0)]
    live = i < nv_ref[0]

    @pl.when(live & ((i == 0) | (e != prev)))
    def _():
        wgu_b[...] = wgu_ref[...].astype(BF)
        wd_b[...] = wd_ref[...].astype(BF)

    @pl.when(live)
    def _():
        h = jnp.dot(x_ref[...].astype(BF), wgu_b[...], preferred_element_type=F32) + bgu_ref[...]
        hg = jnp.minimum(h[:, :D_FF], SWIGLU_LIMIT)
        hl = jnp.clip(h[:, D_FF:], -SWIGLU_LIMIT, SWIGLU_LIMIT)
        a = hg * _sigmoid(SWIGLU_ALPHA * hg) * (hl + 1.0)
        o_ref[...] = jnp.dot(a.astype(BF), wd_b[...], preferred_element_type=F32) + bd_ref[...]

    @pl.when(jnp.logical_not(live))
    def _():
        o_ref[...] = jnp.zeros(o_ref.shape, F32)


def _experts(layer, blk_e, n_live, xb, w_gate_up, b_gate_up, w_down, b_down):
    cap, d = xb.shape
    n_blocks = cap // MOE_BLOCK
    grid_spec = pltpu.PrefetchScalarGridSpec(
        num_scalar_prefetch=2, grid=(n_blocks,),
        in_specs=[pl.BlockSpec((MOE_BLOCK, d), lambda i, be, nv: (i, 0)),
                  pl.BlockSpec((None, None, d, 2 * D_FF), lambda i, be, nv: (layer, be[i], 0, 0)),
                  pl.BlockSpec((None, None, 1, 2 * D_FF), lambda i, be, nv: (layer, be[i], 0, 0)),
                  pl.BlockSpec((None, None, D_FF, d), lambda i, be, nv: (layer, be[i], 0, 0)),
                  pl.BlockSpec((None, None, 1, d), lambda i, be, nv: (layer, be[i], 0, 0))],
        out_specs=pl.BlockSpec((MOE_BLOCK, d), lambda i, be, nv: (i, 0)),
        scratch_shapes=[pltpu.VMEM((d, 2 * D_FF), BF), pltpu.VMEM((D_FF, d), BF)])
    return pl.pallas_call(
        _experts_kernel,
        grid_spec=grid_spec,
        out_shape=jax.ShapeDtypeStruct((cap, d), F32),
        compiler_params=_params(1),
        name="experts",
    )(blk_e, n_live, xb, w_gate_up, b_gate_up.reshape(DEPTH, N_EXPERTS, 1, 2 * D_FF),
      w_down, b_down.reshape(DEPTH, N_EXPERTS, 1, d))


def _moe(layer, h, router_w_pad, router_b_pad, w_gate_up, b_gate_up, w_down, b_down, ln_g, ln_b):
    n, d = h.shape
    idx, gate = _router(h, router_w_pad, router_b_pad, tm=512)
    top_i = idx[:, :TOP_K]
    n_flat = n * TOP_K
    key_bits = (n_flat - 1).bit_length()
    flat_e = top_i.reshape(-1)
    flat_i = jnp.arange(n_flat, dtype=jnp.int32)
    sorted_keys = jnp.sort(flat_e * (1 << key_bits) + flat_i)
    st = (sorted_keys & ((1 << key_bits) - 1)) // TOP_K
    chunk = 256
    experts = jnp.arange(N_EXPERTS, dtype=jnp.int32)
    onehot = (flat_e[:, None] == experts[None, :]).reshape(n_flat // chunk, chunk, N_EXPERTS)
    within = jnp.einsum('ij,cjk->cik', jnp.tril(jnp.ones((chunk, chunk), BF)), onehot.astype(BF),
                        preferred_element_type=F32)
    chunk_tot = within[:, -1, :]
    chunk_off = jnp.cumsum(chunk_tot, axis=0) - chunk_tot
    counts = (chunk_off[-1] + chunk_tot[-1]).astype(jnp.int32)
    padded = (counts + MOE_BLOCK - 1) // MOE_BLOCK * MOE_BLOCK
    pad_end = jnp.cumsum(padded)
    pad_start = pad_end - padded
    start = jnp.cumsum(counts) - counts
    pos = jnp.sum(jnp.where(onehot, within + chunk_off[:, None, :] - 1.0
                            + pad_start.astype(F32)[None, None, :], 0.0),
                  axis=-1).astype(jnp.int32).reshape(n, TOP_K)
    n_blocks = n_flat // MOE_BLOCK + N_EXPERTS
    blk_row0 = jnp.arange(n_blocks, dtype=jnp.int32) * MOE_BLOCK
    blk_e = jnp.minimum(jnp.sum((pad_end[None, :] <= blk_row0[:, None]).astype(jnp.int32), axis=1),
                        N_EXPERTS - 1)
    blk_hot = blk_e[:, None] == experts[None, :]
    blk_off = blk_row0 - jnp.sum(jnp.where(blk_hot, pad_start[None, :], 0), axis=1)
    blk_cnt = jnp.sum(jnp.where(blk_hot, counts[None, :], 0), axis=1)
    blk_src = jnp.sum(jnp.where(blk_hot, start[None, :], 0), axis=1) + blk_off
    n_live = (pad_end[-1:] // MOE_BLOCK).astype(jnp.int32)
    lane_row = jnp.arange(MOE_BLOCK, dtype=jnp.int32)[None, :]
    row_ok = blk_off[:, None] + lane_row < blk_cnt[:, None]
    src = jnp.clip(blk_src[:, None] + lane_row, 0, n_flat - 1)
    buf_t = jnp.where(row_ok, st[src.reshape(-1)].reshape(n_blocks, MOE_BLOCK), 0).reshape(-1)
    xb = h[buf_t]
    yb = _experts(layer, blk_e, n_live, xb, w_gate_up, b_gate_up, w_down, b_down)
    return _combine_ln([yb[pos[:, k]] for k in range(TOP_K)], gate, h, ln_g, ln_b, tm=512,
                       name="moe_out")


def kernel(x_prompt, x_sample, mem_prompt, cache_moba_k, cache_moba_v, cache_diff_k, cache_diff_v, cache_mem_k, cache_mem_v, state_conv, page_table, w_in, b_in, conv_w, conv_b, conv_ln_g, conv_ln_b, lam_q1, lam_k1, lam_q2, lam_k2, diff_norm_g, w_branch, w_out, ln1_g, ln1_b, xq_w, xk_w, xv_w, xo_w, ln2_g, ln2_b, router_w, router_b, w_gate_up, b_gate_up, w_down, b_down, ln3_g, ln3_b):
    bp, seq, d = x_prompt.shape
    db, dseq, _ = x_sample.shape
    n_p = bp * seq
    n_s = db * dseq
    n_pool = cache_moba_k.shape[1]
    hist_pad = 32

    o_qa, o_ka, o_va = 0, MIX_A, 2 * MIX_A
    o_u = 3 * MIX_A
    o_qc = o_u + 2 * C_B
    o_kc, o_vc = o_qc + MIX_C, o_qc + 2 * MIX_C
    o_g = o_qc + 3 * MIX_C

    def permute_cols(a):
        return jnp.concatenate([a[..., o_g:], a[..., o_qc:o_g], a[..., o_qa:o_u], a[..., o_u:o_qc]], axis=-1)

    pools = [jnp.transpose(c, (0, 1, 3, 4, 2)).reshape(DEPTH, n_pool, MIX_A, PAGE_SIZE)
             for c in (cache_moba_k, cache_moba_v)]
    pools += [c.reshape(DEPTH, n_pool, PAGE_SIZE * H_C, 2 * DH_C) for c in (cache_diff_k, cache_diff_v)]
    def mem_rows(c):
        c = c.reshape(DEPTH, db, MEM_LEN, XH, XDH // LANES, LANES)
        return jnp.transpose(c, (0, 1, 2, 4, 3, 5)).reshape(DEPTH, db, MEM_LEN * XH * XDH // LANES, LANES)

    mem_k_s = mem_rows(cache_mem_k)
    mem_v_s = mem_rows(cache_mem_v)
    mem2d = mem_prompt.reshape(bp * MEM_LEN, d)
    hist_p = jnp.zeros((bp, hist_pad, C_B), F32)
    hist_s = jnp.pad(state_conv, ((0, 0), (0, 0), (hist_pad - (CONV_W - 1), 0), (0, 0)))
    router_w_pad = jnp.pad(router_w, ((0, 0), (0, 0), (0, LANES - N_EXPERTS)))
    router_b_pad = jnp.pad(router_b, ((0, 0), (0, LANES - N_EXPERTS))).reshape(DEPTH, 1, LANES)

    x = jnp.concatenate([x_prompt.reshape(n_p, d), x_sample.reshape(n_s, d)], axis=0)
    oa, ob, oc, xo = (jnp.zeros((n_p + n_s, w), F32) for w in (MIX_A, C_B, MIX_C, d))
    outs = {k: [] for k in ("ka_p", "va_p", "kc_p", "vc_p", "mk_p", "mv_p", "cs_p",
                            "ka_s", "va_s", "kc_s", "vc_s", "cs_s")}

    for l in range(DEPTH):
        lam_init = 0.8 - 0.6 * math.exp(-0.3 * l)
        lam_vecs = [v[l].reshape(1, DH_C) for v in (lam_q1, lam_k1, lam_q2, lam_k2)]
        w_in_l = permute_cols(w_in[l]).astype(BF)
        b_in_l = permute_cols(b_in[l])
        wa = w_branch[l, :MIX_A].astype(BF)
        wb = w_branch[l, MIX_A:MIX_A + C_B].astype(BF)
        wc = w_branch[l, MIX_A + C_B:].astype(BF)

        proj = _matmul(x, w_in_l, b_in_l, tm=512, tn=IN_W // 2, name="proj_in")
        proj_s = proj[n_p:]
        new = {k: proj_s[:, c:c + w] for k, c, w in
               (("qa", COL_QA, MIX_A), ("ka", COL_KA, MIX_A), ("va", COL_VA, MIX_A),
                ("qc", COL_QC, MIX_C), ("kc", COL_KC, MIX_C), ("vc", COL_VC, MIX_C))}

        oa = _moba_prompt(proj, bp, seq, oa)
        oc = _diff_prompt(proj, lam_vecs, diff_norm_g[l], bp, seq, lam_init, oc)
        oa, oc = _sample_attn(l, page_table, new["qa"], new["ka"], new["va"],
                              new["qc"], new["kc"], new["vc"], pools, lam_vecs,
                              diff_norm_g[l], lam_init, oa, oc, n_p)
        ob, cs_p = _conv_module(proj, 0, bp, seq, 1, hist_p, conv_w[l], conv_b[l],
                                conv_ln_g[l], conv_ln_b[l], "conv_prompt", ob)
        ob, cs_s = _conv_module(proj, n_p, db, dseq, 16, hist_s[l], conv_w[l], conv_b[l],
                                conv_ln_g[l], conv_ln_b[l], "conv_sample", ob)
        h1 = _merge(oa, ob, oc, proj, x, wa, wb, wc, w_out[l].astype(BF), ln1_g[l], ln1_b[l], tm=256)

        mk_p = _matmul(mem2d, xk_w[l].astype(BF), tm=512, tn=d, name="mem_k")
        mv_p = _matmul(mem2d, xv_w[l].astype(BF), tm=512, tn=d, name="mem_v")
        q = _matmul(h1, xq_w[l].astype(BF), tm=1024, tn=d, name="xattn_q")
        xo = _xattn(q, 0, n_p, mk_p.reshape(bp, MEM_LEN, d), mv_p.reshape(bp, MEM_LEN, d),
                    l, 1, 512, seq // 512, "xattn_prompt", xo)
        xo = _xattn(q, n_p, n_s, mem_k_s, mem_v_s, l, 4, dseq, 1, "xattn_sample", xo)
        h2 = _mm_res_ln(xo, xo_w[l].astype(BF), h1, ln2_g[l], ln2_b[l], tm=512, name="xattn_out")

        x = _moe(l, h2, router_w_pad[l], router_b_pad[l], w_gate_up, b_gate_up, w_down, b_down,
                 ln3_g[l], ln3_b[l])

        proj_p = proj[:n_p]
        outs["ka_p"].append(proj_p[:, COL_KA:COL_KA + MIX_A].reshape(bp, seq, H_A, DH_A))
        outs["va_p"].append(proj_p[:, COL_VA:COL_VA + MIX_A].reshape(bp, seq, H_A, DH_A))
        outs["kc_p"].append(proj_p[:, COL_KC:COL_KC + MIX_C].reshape(bp, seq, H_C, 2 * DH_C))
        outs["vc_p"].append(proj_p[:, COL_VC:COL_VC + MIX_C].reshape(bp, seq, H_C, 2 * DH_C))
        outs["mk_p"].append(mk_p.reshape(bp, MEM_LEN, XH, XDH))
        outs["mv_p"].append(mv_p.reshape(bp, MEM_LEN, XH, XDH))
        outs["cs_p"].append(cs_p)
        outs["ka_s"].append(new["ka"].reshape(db, dseq, H_A, DH_A))
        outs["va_s"].append(new["va"].reshape(db, dseq, H_A, DH_A))
        outs["kc_s"].append(new["kc"].reshape(db, dseq, H_C, 2 * DH_C))
        outs["vc_s"].append(new["vc"].reshape(db, dseq, H_C, 2 * DH_C))
        outs["cs_s"].append(cs_s)

    return (x[:n_p].reshape(bp, seq, d), x[n_p:].reshape(db, dseq, d),
            jnp.stack(outs["ka_p"]), jnp.stack(outs["va_p"]), jnp.stack(outs["kc_p"]),
            jnp.stack(outs["vc_p"]), jnp.stack(outs["mk_p"]), jnp.stack(outs["mv_p"]),
            jnp.stack(outs["cs_p"]), jnp.stack(outs["ka_s"]), jnp.stack(outs["va_s"]),
            jnp.stack(outs["kc_s"]), jnp.stack(outs["vc_s"]), jnp.stack(outs["cs_s"]))
```

```python
import functools
import math

import jax
import jax.numpy as jnp
from jax import lax
from jax.experimental import pallas as pl
from jax.experimental.pallas import tpu as pltpu

D_MODEL = 1024
DEPTH = 2
PAST_LEN = 2048
PAGE_SIZE = 128
N_PAGES = PAST_LEN // PAGE_SIZE

H_A = 4
DH_A = 64
MOBA_BLOCK = 256
MOBA_TOPK = 3
C_B = 256
CONV_W = 31
H_C = 4
DH_C = 64
MEM_LEN = 256
XH = 4
XDH = D_MODEL // XH
N_EXPERTS = 32
TOP_K = 4
D_FF = D_MODEL
SWIGLU_LIMIT = 7.0
SWIGLU_ALPHA = 1.702
MIX_A = H_A * DH_A
MIX_C = H_C * 2 * DH_C
N_BRANCH = 3
DN_ALPHA = (2 * DEPTH) ** 0.25
LN_EPS = 1e-5
LOG2E = math.log2(math.e)

SLOPES_C = tuple(2.0 ** -(i + 1) for i in range(0, 4))
SLOPES_A = tuple(2.0 ** -(i + 1) for i in range(4, 8))

COL_G = 0
COL_QC = 3 * D_MODEL
COL_KC = COL_QC + MIX_C
COL_VC = COL_KC + MIX_C
COL_QA = COL_VC + MIX_C
COL_KA = COL_QA + MIX_A
COL_VA = COL_KA + MIX_A
COL_UA = COL_VA + MIX_A
COL_UB = COL_UA + C_B
IN_W = COL_UB + C_B

LANES = 128
SUBLANES = 8
Q_TILE = MOBA_BLOCK
KEY_TILE = MOBA_BLOCK
MOE_BLOCK = 256
VMEM_LIMIT = 56 * 1024 * 1024

_NT = (((1,), (1,)), ((), ()))
_NEG = -jnp.inf
BF = jnp.bfloat16
F32 = jnp.float32


def _params(n_grid):
    return pltpu.CompilerParams(dimension_semantics=("arbitrary",) * n_grid,
                                vmem_limit_bytes=VMEM_LIMIT)


def _sigmoid(x):
    return 1.0 / (1.0 + jnp.exp(-x))


def _layernorm(x, g, b):
    mu = jnp.mean(x, axis=-1, keepdims=True)
    xc = x - mu
    var = jnp.mean(xc * xc, axis=-1, keepdims=True)
    return xc * lax.rsqrt(var + LN_EPS) * g + b


def _top_picks(g, pos, k, axis=-1):
    lane = pos.astype(F32)
    sel = jnp.zeros(g.shape, F32)
    vals, idxs = [], []
    for _ in range(k):
        m = jnp.max(g, axis=axis, keepdims=True)
        cand = jnp.where(g == m, lane, float(g.shape[axis]))
        idx = jnp.where(m > _NEG, jnp.min(cand, axis=axis, keepdims=True), -1.0)
        pick = lane == idx
        sel = jnp.where(pick, 1.0, sel)
        g = jnp.where(pick, _NEG, g)
        vals.append(m)
        idxs.append(idx)
    return vals, idxs, sel


def _mm_kernel(x_ref, w_ref, *rest, has_bias):
    o_ref = rest[-1]
    acc = jnp.dot(x_ref[...].astype(BF), w_ref[...], preferred_element_type=F32)
    if has_bias:
        acc = acc + rest[0][...]
    o_ref[...] = acc


def _matmul(x, w, b=None, *, tm, tn, name):
    m, k = x.shape
    n = w.shape[1]
    in_specs = [pl.BlockSpec((tm, k), lambda j, i: (i, 0)),
                pl.BlockSpec((k, tn), lambda j, i: (0, j))]
    args = [x, w]
    if b is not None:
        in_specs.append(pl.BlockSpec((1, tn), lambda j, i: (0, j)))
        args.append(b.reshape(1, n))
    return pl.pallas_call(
        functools.partial(_mm_kernel, has_bias=b is not None),
        grid=(n // tn, pl.cdiv(m, tm)),
        in_specs=in_specs,
        out_specs=pl.BlockSpec((tm, tn), lambda j, i: (i, j)),
        out_shape=jax.ShapeDtypeStruct((m, n), F32),
        compiler_params=_params(2),
        name=name,
    )(*args)


def _mm_res_ln_kernel(x_ref, w_ref, r_ref, g_ref, b_ref, o_ref):
    y = jnp.dot(x_ref[...].astype(BF), w_ref[...], preferred_element_type=F32)
    o_ref[...] = _layernorm(DN_ALPHA * r_ref[...] + y, g_ref[...], b_ref[...])


def _mm_res_ln(x, w, res, g, b, *, tm, name):
    m, k = x.shape
    n = w.shape[1]
    return pl.pallas_call(
        _mm_res_ln_kernel,
        grid=(pl.cdiv(m, tm),),
        in_specs=[pl.BlockSpec((tm, k), lambda i: (i, 0)),
                  pl.BlockSpec((k, n), lambda i: (0, 0)),
                  pl.BlockSpec((tm, n), lambda i: (i, 0)),
                  pl.BlockSpec((1, n), lambda i: (0, 0)),
                  pl.BlockSpec((1, n), lambda i: (0, 0))],
        out_specs=pl.BlockSpec((tm, n), lambda i: (i, 0)),
        out_shape=jax.ShapeDtypeStruct((m, n), F32),
        compiler_params=_params(1),
        name=name,
    )(x, w, res, g.reshape(1, n), b.reshape(1, n))


def _combine_ln_kernel(*refs):
    ys, (w_ref, r_ref, g_ref, b_ref, o_ref) = refs[:TOP_K], refs[TOP_K:]
    y = ys[0][...] * w_ref[:, 0:1]
    for k in range(1, TOP_K):
        y = y + ys[k][...] * w_ref[:, k:k + 1]
    o_ref[...] = _layernorm(DN_ALPHA * r_ref[...] + y, g_ref[...], b_ref[...])


def _combine_ln(ys, weights, res, g, b, *, tm, name):
    m, n = res.shape
    row_spec = pl.BlockSpec((tm, n), lambda i: (i, 0))
    return pl.pallas_call(
        _combine_ln_kernel,
        grid=(pl.cdiv(m, tm),),
        in_specs=[row_spec] * TOP_K + [pl.BlockSpec((tm, LANES), lambda i: (i, 0)), row_spec,
                                       pl.BlockSpec((1, n), lambda i: (0, 0)),
                                       pl.BlockSpec((1, n), lambda i: (0, 0))],
        out_specs=row_spec,
        out_shape=jax.ShapeDtypeStruct((m, n), F32),
        compiler_params=_params(1),
        name=name,
    )(*ys, weights, res, g.reshape(1, n), b.reshape(1, n))


def _flash_block(chains, carry):
    ss = []
    for k, q, bias, mask, _, _ in chains:
        s = lax.dot_general(k, q, _NT, preferred_element_type=F32) + bias
        ss.append(s if mask is None else jnp.where(mask, s, _NEG))
    stats = []
    for i, (_, _, _, _, shift, _) in enumerate(chains):
        top = jnp.max(ss[i], axis=0, keepdims=True)
        if carry is None:
            m_new, a = top, None
            p = jnp.exp2(ss[i] - m_new)
            l = jnp.sum(p, axis=0, keepdims=True)
        else:
            m, l_old, _ = carry[i]
            m_new = jnp.maximum(m, top - shift)
            a = jnp.exp2(m - m_new)
            p = jnp.exp2(ss[i] - (m_new + shift))
            l = a * l_old + jnp.sum(p, axis=0, keepdims=True)
        stats.append((m_new, l, a, p.astype(BF)))
    out = []
    for i, (_, _, _, _, _, v_t) in enumerate(chains):
        m_new, l, a, p = stats[i]
        pv = jnp.dot(v_t, p, preferred_element_type=F32)
        out.append((m_new, l, pv if carry is None else a * carry[i][2] + pv))
    return tuple(out)


def _tile_geometry():
    key_j = lax.broadcasted_iota(jnp.int32, (KEY_TILE, Q_TILE), 0)
    qry_i = lax.broadcasted_iota(jnp.int32, (KEY_TILE, Q_TILE), 1)
    causal = qry_i >= key_j
    rel = (key_j - (KEY_TILE - 1)).astype(F32)
    return causal, rel


def _stage_kv(k_ref, v_ref, kb_ref, vt_ref, n_blocks):
    kb_ref[...] = k_ref[...].astype(BF)
    for n in range(n_blocks):
        vt_ref[n] = v_ref[n * KEY_TILE:(n + 1) * KEY_TILE, :].T.astype(BF)


def _moba_prompt_kernel(q_ref, k_ref, v_ref, into_ref, o_ref, kb_ref, vt_ref, km_ref, *, n_blocks):
    del into_ref
    cur = pl.program_id(1)

    @pl.when(cur == 0)
    def _():
        _stage_kv(k_ref, v_ref, kb_ref, vt_ref, n_blocks)
        km_ref[...] = jnp.zeros(km_ref.shape, F32)
        for n in range(n_blocks):
            km_ref[n:n + 1, :] = jnp.mean(k_ref[n * MOBA_BLOCK:(n + 1) * MOBA_BLOCK, :],
                                          axis=0, keepdims=True)

    causal, rel = _tile_geometry()
    q = q_ref[...]
    lane_q = lax.broadcasted_iota(jnp.int32, q.shape, 1)
    lane_p = lax.broadcasted_iota(jnp.int32, (Q_TILE, LANES), 1)
    blk_row = lax.broadcasted_iota(jnp.int32, (km_ref.shape[0], Q_TILE), 0)

    consts = []
    for h in range(H_A):
        pair, hh = divmod(h, 2)
        cols = slice(pair * LANES, (pair + 1) * LANES)
        slope2 = SLOPES_A[h] * LOG2E
        gate = lax.dot_general(km_ref[...], jnp.where(lane_q // DH_A == h, q, 0.0), _NT,
                               precision=lax.Precision.HIGHEST,
                               preferred_element_type=F32)
        gate = jnp.where(blk_row < cur, gate, _NEG)
        _, _, sel = _top_picks(gate, blk_row, MOBA_TOPK, axis=0)
        qm = jnp.where(lane_p // DH_A == hh, q[:, cols] * (DH_A ** -0.5 * LOG2E), 0.0).astype(BF)
        consts.append((cols, slope2, sel, qm, slope2 * rel))

    def block(n, carry):
        start = pl.multiple_of(n * KEY_TILE, KEY_TILE)
        chains = []
        for cols, slope2, sel, qm, bias in consts:
            if carry is None:
                mask, shift = causal, None
            else:
                mask = jnp.max(jnp.where(blk_row == n, sel, 0.0), axis=0, keepdims=True) > 0.5
                shift = slope2 * jnp.full((1, Q_TILE), (cur - n) * KEY_TILE, jnp.int32).astype(F32)
            chains.append((kb_ref[pl.ds(start, KEY_TILE), cols], qm, bias, mask, shift,
                           vt_ref[n, cols, :]))
        return _flash_block(chains, carry)

    final = lax.fori_loop(0, cur, block, block(cur, None))
    row = lax.broadcasted_iota(jnp.int32, (LANES, Q_TILE), 0)
    for pair in range(H_A // 2):
        (_, l0, a0), (_, l1, a1) = final[2 * pair], final[2 * pair + 1]
        o_t = jnp.where(row < DH_A, a0 / l0, a1 / l1)
        o_ref[:, pair * LANES:(pair + 1) * LANES] = o_t.T


def _moba_prompt(proj, batch, seq, into):
    n_q = seq // Q_TILE
    n_blocks = seq // MOBA_BLOCK
    km_rows = -(-n_blocks // SUBLANES) * SUBLANES
    cq, ck, cv = COL_QA // MIX_A, COL_KA // MIX_A, COL_VA // MIX_A
    return pl.pallas_call(
        functools.partial(_moba_prompt_kernel, n_blocks=n_blocks),
        grid=(batch, n_q),
        in_specs=[pl.BlockSpec((Q_TILE, MIX_A), lambda b, i: (b * n_q + i, cq)),
                  pl.BlockSpec((seq, MIX_A), lambda b, i: (b, ck)),
                  pl.BlockSpec((seq, MIX_A), lambda b, i: (b, cv)),
                  pl.BlockSpec(memory_space=pl.ANY)],
        out_specs=pl.BlockSpec((Q_TILE, MIX_A), lambda b, i: (b * n_q + i, 0)),
        out_shape=jax.ShapeDtypeStruct(into.shape, F32),
        input_output_aliases={3: 0},
        scratch_shapes=[pltpu.VMEM((seq, MIX_A), BF),
                        pltpu.VMEM((n_blocks, MIX_A, KEY_TILE), BF),
                        pltpu.VMEM((km_rows, MIX_A), F32)],
        compiler_params=_params(2),
        name="moba_prompt",
    )(proj, proj, proj, into)


def _lambda(lq1, lk1, lq2, lk2, lam_init):
    return (jnp.exp(jnp.sum(lq1[...] * lk1[...], axis=-1, keepdims=True))
            - jnp.exp(jnp.sum(lq2[...] * lk2[...], axis=-1, keepdims=True)) + lam_init)


def _diff_prompt_kernel(q_ref, k_ref, v_ref, lq1, lk1, lq2, lk2, ng_ref, into_ref, o_ref, kb_ref,
                        vt_ref, *, lam_init, n_blocks):
    del into_ref
    cur = pl.program_id(1)

    @pl.when(cur == 0)
    def _():
        _stage_kv(k_ref, v_ref, kb_ref, vt_ref, n_blocks)

    causal, rel = _tile_geometry()
    lam = _lambda(lq1, lk1, lq2, lk2, lam_init)
    lane = lax.broadcasted_iota(jnp.int32, (Q_TILE, LANES), 1)

    consts = []
    for h in range(H_C):
        cols = slice(h * LANES, (h + 1) * LANES)
        slope2 = SLOPES_C[h] * LOG2E
        qh = q_ref[:, cols] * (DH_C ** -0.5 * LOG2E)
        consts.append((cols, slope2, jnp.where(lane < DH_C, qh, 0.0).astype(BF),
                       jnp.where(lane >= DH_C, qh, 0.0).astype(BF), slope2 * rel))

    def block(n, carry):
        start = pl.multiple_of(n * KEY_TILE, KEY_TILE)
        chains = []
        for cols, slope2, q1, q2, bias in consts:
            kb = kb_ref[pl.ds(start, KEY_TILE), cols]
            v_t = vt_ref[n, cols, :]
            if carry is None:
                mask, shift = causal, None
            else:
                mask = None
                shift = slope2 * jnp.full((1, Q_TILE), (cur - n) * KEY_TILE, jnp.int32).astype(F32)
            chains += [(kb, q1, bias, mask, shift, v_t), (kb, q2, bias, mask, shift, v_t)]
        return _flash_block(chains, carry)

    final = lax.fori_loop(0, cur, block, block(cur, None))
    for h in range(H_C):
        (_, l1, a1), (_, l2, a2) = final[2 * h], final[2 * h + 1]
        o_t = a1 / l1 - lam * (a2 / l2)
        o_t = o_t * lax.rsqrt(jnp.mean(o_t * o_t, axis=0, keepdims=True) + LN_EPS) * ng_ref[...]
        o_ref[:, h * LANES:(h + 1) * LANES] = (o_t * (1.0 - lam_init)).T


def _diff_prompt(proj, lam_vecs, norm_g, batch, seq, lam_init, into):
    n_q = seq // Q_TILE
    n_blocks = seq // KEY_TILE
    cq, ck, cv = COL_QC // MIX_C, COL_KC // MIX_C, COL_VC // MIX_C
    vec = pl.BlockSpec((1, DH_C), lambda b, i: (0, 0))
    return pl.pallas_call(
        functools.partial(_diff_prompt_kernel, lam_init=lam_init, n_blocks=n_blocks),
        grid=(batch, n_q),
        in_specs=[pl.BlockSpec((Q_TILE, MIX_C), lambda b, i: (b * n_q + i, cq)),
                  pl.BlockSpec((seq, MIX_C), lambda b, i: (b, ck)),
                  pl.BlockSpec((seq, MIX_C), lambda b, i: (b, cv)),
                  vec, vec, vec, vec,
                  pl.BlockSpec((2 * DH_C, 1), lambda b, i: (0, 0)),
                  pl.BlockSpec(memory_space=pl.ANY)],
        out_specs=pl.BlockSpec((Q_TILE, MIX_C), lambda b, i: (b * n_q + i, 0)),
        out_shape=jax.ShapeDtypeStruct(into.shape, F32),
        input_output_aliases={8: 0},
        scratch_shapes=[pltpu.VMEM((seq, MIX_C), BF),
                        pltpu.VMEM((n_blocks, MIX_C, KEY_TILE), BF)],
        compiler_params=_params(2),
        name="diff_prompt",
    )(proj, proj, proj, *lam_vecs, norm_g.reshape(2 * DH_C, 1), into)


def _sample_attn_kernel(pt_ref, qa_ref, ka_ref, va_ref, qc_ref, kc_ref, vc_ref,
                        lq1, lk1, lq2, lk2, ng_ref, mk_hbm, mv_hbm, dk_hbm, dv_hbm, oa_all, oc_all,
                        oa_ref, oc_ref, mk_buf, mv_buf, dk_buf, dv_buf, sems, *, layer, lam_init, t_new):
    del oa_all, oc_all
    n = N_PAGES
    seq = pl.program_id(0)
    slot = seq % 2
    streams = ((mk_hbm, mk_buf), (mv_hbm, mv_buf), (dk_hbm, dk_buf), (dv_hbm, dv_buf))

    def page_copies(s, to_slot):
        copies = []
        for j in range(n):
            page = pt_ref[s * n + j]
            for p, (pool, buf) in enumerate(streams):
                copies.append(pltpu.make_async_copy(pool.at[layer, page], buf.at[to_slot, j],
                                                    sems.at[to_slot, p]))
        return copies

    @pl.when(seq == 0)
    def _():
        for c in page_copies(seq, slot):
            c.start()

    @pl.when(seq + 1 < pl.num_programs(0))
    def _():
        for c in page_copies(seq + 1, 1 - slot):
            c.start()

    for c in page_copies(seq, slot):
        c.wait()

    mk = [mk_buf.at[slot, j] for j in range(n)]
    mv = [mv_buf.at[slot, j] for j in range(n)]
    dk = [dk_buf.at[slot, j] for j in range(n)]
    dv = [dv_buf.at[slot, j] for j in range(n)]
    pad_rows = PAGE_SIZE - t_new

    def pad_new(x):
        return jnp.concatenate([x, jnp.zeros((pad_rows, x.shape[1]), x.dtype)], axis=0)

    def head_rows(ref):
        return jnp.concatenate([ref[pl.ds(h, PAGE_SIZE, stride=H_C), :] for h in range(H_C)],
                               axis=1)

    def raw_scores(q_rows, page_scores, k_new):
        qb = q_rows.astype(BF)
        tiles = [page_scores(qb, j) for j in range(n)]
        tiles.append(lax.dot_general(qb, pad_new(k_new).astype(BF), _NT, preferred_element_type=F32))
        return tiles

    def softmax_tiles(tiles, slope_col, t_col, gate_sel):
        shape = tiles[0].shape
        lane_f = lax.broadcasted_iota(jnp.int32, shape, 1).astype(F32)
        slope = jnp.broadcast_to(slope_col, shape)
        t_row = jnp.broadcast_to(t_col, shape)
        near = slope * (lane_f - t_row)
        out = []
        for j in range(n):
            s = tiles[j] + (near - slope * float(PAST_LEN - j * PAGE_SIZE))
            if gate_sel is not None:
                s = jnp.where(gate_sel[j // (MOBA_BLOCK // PAGE_SIZE)], s, _NEG)
            out.append(s)
        out.append(jnp.where(lane_f <= t_row, tiles[n] + near, _NEG))
        top = out[0]
        for s in out[1:]:
            top = jnp.maximum(top, s)
        m = jnp.max(top, axis=-1, keepdims=True)
        ps = [jnp.exp(s - m) for s in out]
        tot = ps[0]
        for p in ps[1:]:
            tot = tot + p
        return ps, jnp.sum(tot, axis=-1, keepdims=True)

    def fold_heads(o, head_w, n_heads):
        r = lax.broadcasted_iota(jnp.int32, o.shape, 0)
        c = lax.broadcasted_iota(jnp.int32, o.shape, 1)
        o = jnp.where(r // t_new == c // head_w, o, 0.0)
        out = o[0:t_new]
        for h in range(1, n_heads):
            out = out + o[h * t_new:(h + 1) * t_new]
        return out

    def row_const(rows, values):
        h = lax.broadcasted_iota(jnp.int32, (rows, 1), 0) // t_new
        out = jnp.full((rows, 1), values[-1], F32)
        for i in range(len(values) - 2, -1, -1):
            out = jnp.where(h % len(values) == i, values[i], out)
        return out

    ra = H_A * t_new
    qa = qa_ref[...] * (DH_A ** -0.5)
    qa_rows = jnp.concatenate([qa] * H_A, axis=0)
    r = lax.broadcasted_iota(jnp.int32, qa_rows.shape, 0)
    c = lax.broadcasted_iota(jnp.int32, qa_rows.shape, 1)
    qa_rows = jnp.where(r // t_new == c // DH_A, qa_rows, 0.0)
    row_t = (lax.broadcasted_iota(jnp.int32, (ra, 1), 0) % t_new).astype(F32)
    row_slope = row_const(ra, SLOPES_A)
    pages_per_block = MOBA_BLOCK // PAGE_SIZE
    n_past_blocks = PAST_LEN // MOBA_BLOCK

    rc = H_C * t_new
    qc = qc_ref[...] * (DH_C ** -0.5)
    qc_rows = jnp.concatenate([qc] * (2 * H_C), axis=0)
    r = lax.broadcasted_iota(jnp.int32, qc_rows.shape, 0)
    c = lax.broadcasted_iota(jnp.int32, qc_rows.shape, 1)
    keep = ((r % rc) // t_new == c // (2 * DH_C)) & (r // rc == (c % (2 * DH_C)) // DH_C)
    qc_rows = jnp.where(keep, qc_rows, 0.0)
    row_t_c = (lax.broadcasted_iota(jnp.int32, (2 * rc, 1), 0) % t_new).astype(F32)
    row_slope_c = row_const(2 * rc, SLOPES_C)
    lam = _lambda(lq1, lk1, lq2, lk2, lam_init)

    raw_a = raw_scores(qa_rows,
                       lambda qb, j: jnp.dot(qb, mk[j][...].astype(BF), preferred_element_type=F32),
                       ka_ref[...])
    raw_c = raw_scores(qc_rows,
                       lambda qb, j: lax.dot_general(qb, head_rows(dk[j]).astype(BF), _NT,
                                                     preferred_element_type=F32),
                       kc_ref[...])
    lane_g = lax.broadcasted_iota(jnp.int32, (ra, PAGE_SIZE), 1)
    gate = jnp.full((ra, PAGE_SIZE), _NEG, F32)
    for b in range(n_past_blocks):
        tot = raw_a[b * pages_per_block]
        for j in range(1, pages_per_block):
            tot = tot + raw_a[b * pages_per_block + j]
        gate = jnp.where(lane_g == b, jnp.sum(tot, axis=-1, keepdims=True), gate)
    _, _, sel = _top_picks(gate, lane_g, MOBA_TOPK)
    gate_sel = [jnp.broadcast_to(jnp.max(jnp.where(lane_g == b, sel, 0.0), axis=-1, keepdims=True),
                                 (ra, PAGE_SIZE)) > 0.5 for b in range(n_past_blocks)]
    ps_a, l_a = softmax_tiles(raw_a, row_slope, row_t, gate_sel)
    ps_c, l_c = softmax_tiles(raw_c, row_slope_c, row_t_c, None)
    inv1 = jnp.broadcast_to(1.0 / l_c[0:rc], (rc, PAGE_SIZE))
    inv2 = jnp.broadcast_to(lam / l_c[rc:2 * rc], (rc, PAGE_SIZE))
    ws = [p[0:rc] * inv1 - p[rc:2 * rc] * inv2 for p in ps_c]
    oa = jnp.dot(ps_a[n].astype(BF), pad_new(va_ref[...]).astype(BF), preferred_element_type=F32)
    for j in range(n):
        oa = oa + lax.dot_general(ps_a[j].astype(BF), mv[j][...].astype(BF), _NT,
                                  preferred_element_type=F32)
    oc = jnp.dot(ws[n].astype(BF), pad_new(vc_ref[...]).astype(BF), preferred_element_type=F32)
    for j in range(n):
        oc = oc + jnp.dot(ws[j].astype(BF), head_rows(dv[j]).astype(BF),
                          preferred_element_type=F32)
    oa_ref[...] = fold_heads(oa / l_a, DH_A, H_A)
    oc = fold_heads(oc, 2 * DH_C, H_C)
    for h in range(H_C):
        cols = slice(h * LANES, (h + 1) * LANES)
        o = oc[:, cols]
        o = o * lax.rsqrt(jnp.mean(o * o, axis=-1, keepdims=True) + LN_EPS) * ng_ref[...]
        oc_ref[:, cols] = o * (1.0 - lam_init)


def _sample_attn(layer, page_table, qa, ka, va, qc, kc, vc, pools, lam_vecs, norm_g, lam_init,
                 oa_all, oc_all, row0):
    n_seq = page_table.shape[0]
    t_new = qa.shape[0] // n_seq
    pt_flat = page_table.reshape(-1)

    def new_spec(w):
        return pl.BlockSpec((t_new, w), lambda b, pt: (b, 0))

    vec = pl.BlockSpec((1, DH_C), lambda b, pt: (0, 0))
    in_specs = [new_spec(MIX_A)] * 3 + [new_spec(MIX_C)] * 3 + [vec] * 4
    in_specs.append(pl.BlockSpec((1, 2 * DH_C), lambda b, pt: (0, 0)))
    args = [qa, ka, va, qc, kc, vc, *lam_vecs, norm_g.reshape(1, 2 * DH_C)]
    in_specs += [pl.BlockSpec(memory_space=pl.ANY)] * len(pools)
    args += list(pools)
    blk0 = row0 // t_new
    first_alias = 1 + len(args)
    in_specs += [pl.BlockSpec(memory_space=pl.ANY)] * 2
    args += [oa_all, oc_all]
    n_slots = 2
    grid_spec = pltpu.PrefetchScalarGridSpec(
        num_scalar_prefetch=1, grid=(n_seq,), in_specs=in_specs,
        out_specs=[pl.BlockSpec((t_new, MIX_A), lambda b, pt: (blk0 + b, 0)),
                   pl.BlockSpec((t_new, MIX_C), lambda b, pt: (blk0 + b, 0))],
        scratch_shapes=[pltpu.VMEM((n_slots, N_PAGES) + pool.shape[2:], F32) for pool in pools]
        + [pltpu.SemaphoreType.DMA((n_slots, len(pools)))])
    return pl.pallas_call(
        functools.partial(_sample_attn_kernel, layer=layer, lam_init=lam_init, t_new=t_new),
        grid_spec=grid_spec,
        out_shape=[jax.ShapeDtypeStruct(oa_all.shape, F32), jax.ShapeDtypeStruct(oc_all.shape, F32)],
        input_output_aliases={first_alias: 0, first_alias + 1: 1},
        compiler_params=_params(1),
        name="sample_attn",
    )(pt_flat, *args)


def _conv_kernel(ua_ref, ub_ref, hist_ref, cw_ref, cb_ref, g_ref, b_ref, into_ref, o_ref, st_ref, z_ref,
                 *, n_seq, t_len, chunk):
    del into_ref
    pad = hist_ref.shape[1]
    lead = pad - (CONV_W - 1)
    for s in range(n_seq):
        rows = slice(s * t_len, (s + 1) * t_len)
        z_ref[0:pad, :] = hist_ref[s]
        z_ref[pad:pad + t_len, :] = ua_ref[rows, :] * _sigmoid(ub_ref[rows, :])
        for c0 in range(0, t_len, chunk):
            acc = jnp.zeros((chunk, C_B), F32) + cb_ref[...]
            for w in range(CONV_W):
                lo = c0 + lead + w
                acc = acc + z_ref[lo:lo + chunk, :] * cw_ref[w:w + 1, :]
            y = _layernorm(acc, g_ref[...], b_ref[...])
            o_ref[s * t_len + c0:s * t_len + c0 + chunk, :] = y * _sigmoid(y)
        st_ref[s] = z_ref[lead + t_len:pad + t_len, :]


def _conv_module(proj, row0, n_seq_total, t_len, seq_per_step, hist, conv_w, conv_b, ln_g, ln_b,
                 name, into):
    rows = seq_per_step * t_len
    blk0 = row0 // rows
    pad = hist.shape[1]
    vec = pl.BlockSpec((1, C_B), lambda i: (0, 0))
    return pl.pallas_call(
        functools.partial(_conv_kernel, n_seq=seq_per_step, t_len=t_len, chunk=min(t_len, 256)),
        grid=(n_seq_total // seq_per_step,),
        in_specs=[pl.BlockSpec((rows, C_B), lambda i: (blk0 + i, COL_UA // C_B)),
                  pl.BlockSpec((rows, C_B), lambda i: (blk0 + i, COL_UB // C_B)),
                  pl.BlockSpec((seq_per_step, pad, C_B), lambda i: (i, 0, 0)),
                  pl.BlockSpec((CONV_W, C_B), lambda i: (0, 0)),
                  vec, vec, vec, pl.BlockSpec(memory_space=pl.ANY)],
        out_specs=[pl.BlockSpec((rows, C_B), lambda i: (blk0 + i, 0)),
                   pl.BlockSpec((seq_per_step, CONV_W - 1, C_B), lambda i: (i, 0, 0))],
        out_shape=[jax.ShapeDtypeStruct(into.shape, F32),
                   jax.ShapeDtypeStruct((n_seq_total, CONV_W - 1, C_B), F32)],
        input_output_aliases={7: 0},
        scratch_shapes=[pltpu.VMEM((pad + t_len, C_B), F32)],
        compiler_params=_params(1),
        name=name,
    )(proj, proj, hist, conv_w, conv_b.reshape(1, C_B), ln_g.reshape(1, C_B), ln_b.reshape(1, C_B),
      into)


def _merge_kernel(oa_ref, ob_ref, oc_ref, g0_ref, g1_ref, g2_ref, x_ref, wa_ref, wb_ref, wc_ref,
                  wo_ref, g_ref, b_ref, o_ref):
    def branch(o_r, w_r, gate_r):
        y = jnp.dot(o_r[...].astype(BF), w_r[...], preferred_element_type=F32)
        return _sigmoid(gate_r[...]) * y

    merged = branch(oa_ref, wa_ref, g0_ref) + branch(ob_ref, wb_ref, g1_ref) + branch(oc_ref, wc_ref, g2_ref)
    y = jnp.dot(merged.astype(BF), wo_ref[...], preferred_element_type=F32)
    o_ref[...] = _layernorm(DN_ALPHA * x_ref[...] + y, g_ref[...], b_ref[...])


def _merge(oa, ob, oc, proj, x, wa, wb, wc, wo, g, b, *, tm):
    m = x.shape[0]
    d = D_MODEL

    def rows(w, col=0):
        return pl.BlockSpec((tm, w), lambda i, col=col: (i, col))

    def whole(a):
        return pl.BlockSpec(a.shape, lambda i: (0, 0))

    vec = pl.BlockSpec((1, d), lambda i: (0, 0))
    return pl.pallas_call(
        _merge_kernel,
        grid=(pl.cdiv(m, tm),),
        in_specs=[rows(MIX_A), rows(C_B), rows(MIX_C), rows(d, 0), rows(d, 1), rows(d, 2), rows(d),
                  whole(wa), whole(wb), whole(wc), whole(wo), vec, vec],
        out_specs=rows(d),
        out_shape=jax.ShapeDtypeStruct((m, d), F32),
        compiler_params=_params(1),
        name="merge",
    )(oa, ob, oc, proj, proj, proj, x, wa, wb, wc, wo, g.reshape(1, d), b.reshape(1, d))


def _xattn_kernel(q_ref, mk_ref, mv_ref, into_ref, o_ref, *, n_seq, t_len, interleaved):
    del into_ref
    def head(ref, s, h):
        if interleaved:
            tiles = XDH // LANES
            return jnp.concatenate(
                [ref[s, pl.ds(c * XH + h, MEM_LEN, stride=XH * tiles), :] for c in range(tiles)],
                axis=1).astype(BF)
        return ref[s, :, h * XDH:(h + 1) * XDH].astype(BF)

    for s in range(n_seq):
        rows = slice(s * t_len, (s + 1) * t_len)
        for h in range(XH):
            cols = slice(h * XDH, (h + 1) * XDH)
            qh = (q_ref[rows, cols] * (XDH ** -0.5)).astype(BF)
            sc = lax.dot_general(qh, head(mk_ref, s, h), _NT, preferred_element_type=F32)
            m = jnp.max(sc, axis=-1, keepdims=True)
            p = jnp.exp(sc - m)
            l = jnp.sum(p, axis=-1, keepdims=True)
            o = jnp.dot(p.astype(BF), head(mv_ref, s, h), preferred_element_type=F32)
            o_ref[rows, cols] = o / l


def _xattn(q, row0, n_rows, mem_k, mem_v, layer, seq_per_step, t_len, steps_per_mem, name, into):
    rows = seq_per_step * t_len
    blk0 = row0 // rows
    interleaved = mem_k.ndim == 4
    if interleaved:
        mem_spec = pl.BlockSpec((None, seq_per_step) + mem_k.shape[2:],
                                lambda i: (layer, i // steps_per_mem, 0, 0))
    else:
        mem_spec = pl.BlockSpec((seq_per_step, MEM_LEN, D_MODEL), lambda i: (i // steps_per_mem, 0, 0))
    return pl.pallas_call(
        functools.partial(_xattn_kernel, n_seq=seq_per_step, t_len=t_len, interleaved=interleaved),
        grid=(n_rows // rows,),
        in_specs=[pl.BlockSpec((rows, D_MODEL), lambda i: (blk0 + i, 0)), mem_spec, mem_spec,
                  pl.BlockSpec(memory_space=pl.ANY)],
        out_specs=pl.BlockSpec((rows, D_MODEL), lambda i: (blk0 + i, 0)),
        out_shape=jax.ShapeDtypeStruct(into.shape, F32),
        input_output_aliases={3: 0},
        compiler_params=_params(1),
        name=name,
    )(q, mem_k, mem_v, into)


def _router_kernel(x_ref, w_ref, b_ref, idx_ref, gate_ref):
    logits = jnp.dot(x_ref[...], w_ref[...], precision=lax.Precision.HIGHEST,
                     preferred_element_type=F32) + b_ref[...]
    lane = lax.broadcasted_iota(jnp.int32, logits.shape, 1)
    logits = jnp.where(lane < N_EXPERTS, logits, _NEG)
    vals, idxs, _ = _top_picks(logits, lane, TOP_K)
    es = [jnp.exp(v - vals[0]) for v in vals]
    denom = es[0]
    for e in es[1:]:
        denom = denom + e
    idx_out = jnp.zeros(logits.shape, jnp.int32)
    gate_out = jnp.zeros(logits.shape, F32)
    for k in range(TOP_K):
        idx_out = jnp.where(lane == k, idxs[k].astype(jnp.int32), idx_out)
        gate_out = jnp.where(lane == k, es[k] / denom, gate_out)
    idx_ref[...] = idx_out
    gate_ref[...] = gate_out


def _router(x, w_pad, b_pad, *, tm):
    m, d = x.shape
    return pl.pallas_call(
        _router_kernel,
        grid=(pl.cdiv(m, tm),),
        in_specs=[pl.BlockSpec((tm, d), lambda i: (i, 0)),
                  pl.BlockSpec((d, LANES), lambda i: (0, 0)),
                  pl.BlockSpec((1, LANES), lambda i: (0, 0))],
        out_specs=[pl.BlockSpec((tm, LANES), lambda i: (i, 0))] * 2,
        out_shape=[jax.ShapeDtypeStruct((m, LANES), jnp.int32),
                   jax.ShapeDtypeStruct((m, LANES), F32)],
        compiler_params=_params(1),
        name="router",
    )(x, w_pad, b_pad)


def _experts_kernel(be_ref, nv_ref, first_ref, slot_ref, next_ref, x_ref, wgu_hbm, bgu_ref, wd_hbm, bd_ref,
                    o_ref, wgu_f, wd_f, wgu_b, wd_b, sems, *, layer):
    i = pl.program_id(0)
    live = i < nv_ref[0]

    def weight_copies(e, s):
        return (pltpu.make_async_copy(wgu_hbm.at[layer, e], wgu_f.at[s], sems.at[s, 0]),
                pltpu.make_async_copy(wd_hbm.at[layer, e], wd_f.at[s], sems.at[s, 1]))

    @pl.when(first_ref[i] == 1)
    def _():
        slot = slot_ref[i]

        @pl.when(i == 0)
        def _():
            for c in weight_copies(be_ref[i], slot):
                c.start()

        @pl.when(next_ref[i] >= 0)
        def _():
            for c in weight_copies(next_ref[i], 1 - slot):
                c.start()

        for c in weight_copies(be_ref[i], slot):
            c.wait()
        wgu_b[...] = wgu_f[slot].astype(BF)
        wd_b[...] = wd_f[slot].astype(BF)

    @pl.when(live)
    def _():
        h = jnp.dot(x_ref[...].astype(BF), wgu_b[...], preferred_element_type=F32) + bgu_ref[...]
        hg = jnp.minimum(h[:, :D_FF], SWIGLU_LIMIT)
        hl = jnp.clip(h[:, D_FF:], -SWIGLU_LIMIT, SWIGLU_LIMIT)
        a = hg * _sigmoid(SWIGLU_ALPHA * hg) * (hl + 1.0)
        o_ref[...] = jnp.dot(a.astype(BF), wd_b[...], preferred_element_type=F32) + bd_ref[...]

    @pl.when(jnp.logical_not(live))
    def _():
        o_ref[...] = jnp.zeros(o_ref.shape, F32)


def _experts(layer, blk_e, n_live, xb, w_gate_up, b_gate_up, w_down, b_down):
    cap, d = xb.shape
    n_blocks = cap // MOE_BLOCK
    blk = jnp.arange(n_blocks, dtype=jnp.int32)
    prev_e = jnp.concatenate([jnp.full((1,), -1, jnp.int32), blk_e[:-1]])
    first = (blk < n_live[0]) & (blk_e != prev_e)
    slot = (jnp.cumsum(first.astype(jnp.int32)) - 1) % 2
    first_at = jnp.where(first, blk, n_blocks)
    next_first = jnp.concatenate([lax.cummin(first_at, reverse=True)[1:],
                                  jnp.full((1,), n_blocks, jnp.int32)])
    next_e = jnp.where(next_first < n_blocks, blk_e[jnp.minimum(next_first, n_blocks - 1)], -1)
    n_slots = 2
    hbm = pl.BlockSpec(memory_space=pl.ANY)
    grid_spec = pltpu.PrefetchScalarGridSpec(
        num_scalar_prefetch=5, grid=(n_blocks,),
        in_specs=[pl.BlockSpec((MOE_BLOCK, d), lambda i, be, *_: (i, 0)),
                  hbm,
                  pl.BlockSpec((None, None, 1, 2 * D_FF), lambda i, be, *_: (layer, be[i], 0, 0)),
                  hbm,
                  pl.BlockSpec((None, None, 1, d), lambda i, be, *_: (layer, be[i], 0, 0))],
        out_specs=pl.BlockSpec((MOE_BLOCK, d), lambda i, be, *_: (i, 0)),
        scratch_shapes=[pltpu.VMEM((n_slots, d, 2 * D_FF), F32), pltpu.VMEM((n_slots, D_FF, d), F32),
                        pltpu.VMEM((d, 2 * D_FF), BF), pltpu.VMEM((D_FF, d), BF),
                        pltpu.SemaphoreType.DMA((n_slots, 2))])
    return pl.pallas_call(
        functools.partial(_experts_kernel, layer=layer),
        grid_spec=grid_spec,
        out_shape=jax.ShapeDtypeStruct((cap, d), F32),
        compiler_params=_params(1),
        name="experts",
    )(blk_e, n_live, first.astype(jnp.int32), slot.astype(jnp.int32), next_e.astype(jnp.int32), xb,
      w_gate_up, b_gate_up.reshape(DEPTH, N_EXPERTS, 1, 2 * D_FF),
      w_down, b_down.reshape(DEPTH, N_EXPERTS, 1, d))


def _moe(layer, h, router_w_pad, router_b_pad, w_gate_up, b_gate_up, w_down, b_down, ln_g, ln_b):
    n, d = h.shape
    idx, gate = _router(h, router_w_pad, router_b_pad, tm=512)
    top_i = idx[:, :TOP_K]
    n_flat = n * TOP_K
    key_bits = (n_flat - 1).bit_length()
    flat_e = top_i.reshape(-1)
    flat_i = jnp.arange(n_flat, dtype=jnp.int32)
    sorted_keys = jnp.sort(flat_e * (1 << key_bits) + flat_i)
    st = (sorted_keys & ((1 << key_bits) - 1)) // TOP_K
    chunk = 256
    experts = jnp.arange(N_EXPERTS, dtype=jnp.int32)
    onehot = (flat_e[:, None] == experts[None, :]).reshape(n_flat // chunk, chunk, N_EXPERTS)
    within = jnp.einsum('ij,cjk->cik', jnp.tril(jnp.ones((chunk, chunk), BF)), onehot.astype(BF),
                        preferred_element_type=F32)
    chunk_tot = within[:, -1, :]
    chunk_off = jnp.cumsum(chunk_tot, axis=0) - chunk_tot
    counts = (chunk_off[-1] + chunk_tot[-1]).astype(jnp.int32)
    padded = (counts + MOE_BLOCK - 1) // MOE_BLOCK * MOE_BLOCK
    pad_end = jnp.cumsum(padded)
    pad_start = pad_end - padded
    start = jnp.cumsum(counts) - counts
    pos = jnp.sum(jnp.where(onehot, within + chunk_off[:, None, :] - 1.0
                            + pad_start.astype(F32)[None, None, :], 0.0),
                  axis=-1).astype(jnp.int32).reshape(n, TOP_K)
    n_blocks = n_flat // MOE_BLOCK + N_EXPERTS
    blk_row0 = jnp.arange(n_blocks, dtype=jnp.int32) * MOE_BLOCK
    blk_e = jnp.minimum(jnp.sum((pad_end[None, :] <= blk_row0[:, None]).astype(jnp.int32), axis=1),
                        N_EXPERTS - 1)
    blk_hot = blk_e[:, None] == experts[None, :]
    blk_off = blk_row0 - jnp.sum(jnp.where(blk_hot, pad_start[None, :], 0), axis=1)
    blk_cnt = jnp.sum(jnp.where(blk_hot, counts[None, :], 0), axis=1)
    blk_src = jnp.sum(jnp.where(blk_hot, start[None, :], 0), axis=1) + blk_off
    n_live = (pad_end[-1:] // MOE_BLOCK).astype(jnp.int32)
    lane_row = jnp.arange(MOE_BLOCK, dtype=jnp.int32)[None, :]
    row_ok = blk_off[:, None] + lane_row < blk_cnt[:, None]
    src = jnp.clip(blk_src[:, None] + lane_row, 0, n_flat - 1)
    buf_t = jnp.where(row_ok, st[src.reshape(-1)].reshape(n_blocks, MOE_BLOCK), 0).reshape(-1)
    xb = h[buf_t]
    yb = _experts(layer, blk_e, n_live, xb, w_gate_up, b_gate_up, w_down, b_down)
    return _combine_ln([yb[pos[:, k]] for k in range(TOP_K)], gate, h, ln_g, ln_b, tm=512,
                       name="moe_out")


def kernel(x_prompt, x_sample, mem_prompt, cache_moba_k, cache_moba_v, cache_diff_k, cache_diff_v, cache_mem_k, cache_mem_v, state_conv, page_table, w_in, b_in, conv_w, conv_b, conv_ln_g, conv_ln_b, lam_q1, lam_k1, lam_q2, lam_k2, diff_norm_g, w_branch, w_out, ln1_g, ln1_b, xq_w, xk_w, xv_w, xo_w, ln2_g, ln2_b, router_w, router_b, w_gate_up, b_gate_up, w_down, b_down, ln3_g, ln3_b):
    bp, seq, d = x_prompt.shape
    db, dseq, _ = x_sample.shape
    n_p = bp * seq
    n_s = db * dseq
    n_pool = cache_moba_k.shape[1]
    hist_pad = 32

    o_qa, o_ka, o_va = 0, MIX_A, 2 * MIX_A
    o_u = 3 * MIX_A
    o_qc = o_u + 2 * C_B
    o_kc, o_vc = o_qc + MIX_C, o_qc + 2 * MIX_C
    o_g = o_qc + 3 * MIX_C

    def permute_cols(a):
        return jnp.concatenate([a[..., o_g:], a[..., o_qc:o_g], a[..., o_qa:o_u], a[..., o_u:o_qc]], axis=-1)

    pools = [jnp.transpose(c, (0, 1, 3, 4, 2)).reshape(DEPTH, n_pool, MIX_A, PAGE_SIZE)
             for c in (cache_moba_k, cache_moba_v)]
    pools += [c.reshape(DEPTH, n_pool, PAGE_SIZE * H_C, 2 * DH_C) for c in (cache_diff_k, cache_diff_v)]
    def mem_rows(c):
        c = c.reshape(DEPTH, db, MEM_LEN, XH, XDH // LANES, LANES)
        return jnp.transpose(c, (0, 1, 2, 4, 3, 5)).reshape(DEPTH, db, MEM_LEN * XH * XDH // LANES, LANES)

    mem_k_s = mem_rows(cache_mem_k)
    mem_v_s = mem_rows(cache_mem_v)
    mem2d = mem_prompt.reshape(bp * MEM_LEN, d)
    hist_p = jnp.zeros((bp, hist_pad, C_B), F32)
    hist_s = jnp.pad(state_conv, ((0, 0), (0, 0), (hist_pad - (CONV_W - 1), 0), (0, 0)))
    router_w_pad = jnp.pad(router_w, ((0, 0), (0, 0), (0, LANES - N_EXPERTS)))
    router_b_pad = jnp.pad(router_b, ((0, 0), (0, LANES - N_EXPERTS))).reshape(DEPTH, 1, LANES)

    x = jnp.concatenate([x_prompt.reshape(n_p, d), x_sample.reshape(n_s, d)], axis=0)
    oa, ob, oc, xo = (jnp.zeros((n_p + n_s, w), F32) for w in (MIX_A, C_B, MIX_C, d))
    outs = {k: [] for k in ("ka_p", "va_p", "kc_p", "vc_p", "mk_p", "mv_p", "cs_p",
                            "ka_s", "va_s", "kc_s", "vc_s", "cs_s")}

    for l in range(DEPTH):
        lam_init = 0.8 - 0.6 * math.exp(-0.3 * l)
        lam_vecs = [v[l].reshape(1, DH_C) for v in (lam_q1, lam_k1, lam_q2, lam_k2)]
        w_in_l = permute_cols(w_in[l]).astype(BF)
        b_in_l = permute_cols(b_in[l])
        wa = w_branch[l, :MIX_A].astype(BF)
        wb = w_branch[l, MIX_A:MIX_A + C_B].astype(BF)
        wc = w_branch[l, MIX_A + C_B:].astype(BF)

        proj = _matmul(x, w_in_l, b_in_l, tm=512, tn=IN_W // 2, name="proj_in")
        proj_s = proj[n_p:]
        new = {k: proj_s[:, c:c + w] for k, c, w in
               (("qa", COL_QA, MIX_A), ("ka", COL_KA, MIX_A), ("va", COL_VA, MIX_A),
                ("qc", COL_QC, MIX_C), ("kc", COL_KC, MIX_C), ("vc", COL_VC, MIX_C))}

        oa = _moba_prompt(proj, bp, seq, oa)
        oc = _diff_prompt(proj, lam_vecs, diff_norm_g[l], bp, seq, lam_init, oc)
        oa, oc = _sample_attn(l, page_table, new["qa"], new["ka"], new["va"],
                              new["qc"], new["kc"], new["vc"], pools, lam_vecs,
                              diff_norm_g[l], lam_init, oa, oc, n_p)
        ob, cs_p = _conv_module(proj, 0, bp, seq, 1, hist_p, conv_w[l], conv_b[l],
                                conv_ln_g[l], conv_ln_b[l], "conv_prompt", ob)
        ob, cs_s = _conv_module(proj, n_p, db, dseq, 16, hist_s[l], conv_w[l], conv_b[l],
                                conv_ln_g[l], conv_ln_b[l], "conv_sample", ob)
        h1 = _merge(oa, ob, oc, proj, x, wa, wb, wc, w_out[l].astype(BF), ln1_g[l], ln1_b[l], tm=256)

        mk_p = _matmul(mem2d, xk_w[l].astype(BF), tm=512, tn=d, name="mem_k")
        mv_p = _matmul(mem2d, xv_w[l].astype(BF), tm=512, tn=d, name="mem_v")
        q = _matmul(h1, xq_w[l].astype(BF), tm=1024, tn=d, name="xattn_q")
        xo = _xattn(q, 0, n_p, mk_p.reshape(bp, MEM_LEN, d), mv_p.reshape(bp, MEM_LEN, d),
                    l, 1, 512, seq // 512, "xattn_prompt", xo)
        xo = _xattn(q, n_p, n_s, mem_k_s, mem_v_s, l, 4, dseq, 1, "xattn_sample", xo)
        h2 = _mm_res_ln(xo, xo_w[l].astype(BF), h1, ln2_g[l], ln2_b[l], tm=512, name="xattn_out")

        x = _moe(l, h2, router_w_pad[l], router_b_pad[l], w_gate_up, b_gate_up, w_down, b_down,
                 ln3_g[l], ln3_b[l])

        proj_p = proj[:n_p]
        outs["ka_p"].append(proj_p[:, COL_KA:COL_KA + MIX_A].reshape(bp, seq, H_A, DH_A))
        outs["va_p"].append(proj_p[:, COL_VA:COL_VA + MIX_A].reshape(bp, seq, H_A, DH_A))
        outs["kc_p"].append(proj_p[:, COL_KC:COL_KC + MIX_C].reshape(bp, seq, H_C, 2 * DH_C))
        outs["vc_p"].append(proj_p[:, COL_VC:COL_VC + MIX_C].reshape(bp, seq, H_C, 2 * DH_C))
        outs["mk_p"].append(mk_p.reshape(bp, MEM_LEN, XH, XDH))
        outs["mv_p"].append(mv_p.reshape(bp, MEM_LEN, XH, XDH))
        outs["cs_p"].append(cs_p)
        outs["ka_s"].append(new["ka"].reshape(db, dseq, H_A, DH_A))
        outs["va_s"].append(new["va"].reshape(db, dseq, H_A, DH_A))
        outs["kc_s"].append(new["kc"].reshape(db, dseq, H_C, 2 * DH_C))
        outs["vc_s"].append(new["vc"].reshape(db, dseq, H_C, 2 * DH_C))
        outs["cs_s"].append(cs_s)

    return (x[:n_p].reshape(bp, seq, d), x[n_p:].reshape(db, dseq, d),
            jnp.stack(outs["ka_p"]), jnp.stack(outs["va_p"]), jnp.stack(outs["kc_p"]),
            jnp.stack(outs["vc_p"]), jnp.stack(outs["mk_p"]), jnp.stack(outs["mv_p"]),
            jnp.stack(outs["cs_p"]), jnp.stack(outs["ka_s"]), jnp.stack(outs["va_s"]),
            jnp.stack(outs["kc_s"]), jnp.stack(outs["vc_s"]), jnp.stack(outs["cs_s"]))
```

```python
import functools
import math

import jax
import jax.numpy as jnp
from jax import lax
from jax.experimental import pallas as pl
from jax.experimental.pallas import tpu as pltpu

D_MODEL = 1024
DEPTH = 2
PAST_LEN = 2048
PAGE_SIZE = 128
N_PAGES = PAST_LEN // PAGE_SIZE

H_A = 4
DH_A = 64
MOBA_BLOCK = 256
MOBA_TOPK = 3
C_B = 256
CONV_W = 31
H_C = 4
DH_C = 64
MEM_LEN = 256
XH = 4
XDH = D_MODEL // XH
N_EXPERTS = 32
TOP_K = 4
D_FF = D_MODEL
SWIGLU_LIMIT = 7.0
SWIGLU_ALPHA = 1.702
MIX_A = H_A * DH_A
MIX_C = H_C * 2 * DH_C
N_BRANCH = 3
DN_ALPHA = (2 * DEPTH) ** 0.25
LN_EPS = 1e-5
LOG2E = math.log2(math.e)

SLOPES_C = tuple(2.0 ** -(i + 1) for i in range(0, 4))
SLOPES_A = tuple(2.0 ** -(i + 1) for i in range(4, 8))

COL_G = 0
COL_QC = 3 * D_MODEL
COL_KC = COL_QC + MIX_C
COL_VC = COL_KC + MIX_C
COL_QA = COL_VC + MIX_C
COL_KA = COL_QA + MIX_A
COL_VA = COL_KA + MIX_A
COL_UA = COL_VA + MIX_A
COL_UB = COL_UA + C_B
IN_W = COL_UB + C_B

LANES = 128
SUBLANES = 8
Q_TILE = MOBA_BLOCK
KEY_TILE = MOBA_BLOCK
MOE_BLOCK = 256
VMEM_LIMIT = 56 * 1024 * 1024

_NT = (((1,), (1,)), ((), ()))
_NEG = -jnp.inf
BF = jnp.bfloat16
F32 = jnp.float32


def _params(n_grid):
    return pltpu.CompilerParams(dimension_semantics=("arbitrary",) * n_grid,
                                vmem_limit_bytes=VMEM_LIMIT)


def _sigmoid(x):
    return 1.0 / (1.0 + jnp.exp(-x))


def _layernorm(x, g, b):
    mu = jnp.mean(x, axis=-1, keepdims=True)
    xc = x - mu
    var = jnp.mean(xc * xc, axis=-1, keepdims=True)
    return xc * lax.rsqrt(var + LN_EPS) * g + b


def _top_picks(g, pos, k, axis=-1):
    lane = pos.astype(F32)
    sel = jnp.zeros(g.shape, F32)
    vals, idxs = [], []
    for _ in range(k):
        m = jnp.max(g, axis=axis, keepdims=True)
        cand = jnp.where(g == m, lane, float(g.shape[axis]))
        idx = jnp.where(m > _NEG, jnp.min(cand, axis=axis, keepdims=True), -1.0)
        pick = lane == idx
        sel = jnp.where(pick, 1.0, sel)
        g = jnp.where(pick, _NEG, g)
        vals.append(m)
        idxs.append(idx)
    return vals, idxs, sel


def _mm_kernel(x_ref, w_ref, *rest, has_bias):
    o_ref = rest[-1]
    acc = jnp.dot(x_ref[...].astype(BF), w_ref[...], preferred_element_type=F32)
    if has_bias:
        acc = acc + rest[0][...]
    o_ref[...] = acc


def _matmul(x, w, b=None, *, tm, tn, name):
    m, k = x.shape
    n = w.shape[1]
    in_specs = [pl.BlockSpec((tm, k), lambda j, i: (i, 0)),
                pl.BlockSpec((k, tn), lambda j, i: (0, j))]
    args = [x, w]
    if b is not None:
        in_specs.append(pl.BlockSpec((1, tn), lambda j, i: (0, j)))
        args.append(b.reshape(1, n))
    return pl.pallas_call(
        functools.partial(_mm_kernel, has_bias=b is not None),
        grid=(n // tn, pl.cdiv(m, tm)),
        in_specs=in_specs,
        out_specs=pl.BlockSpec((tm, tn), lambda j, i: (i, j)),
        out_shape=jax.ShapeDtypeStruct((m, n), F32),
        compiler_params=_params(2),
        name=name,
    )(*args)


def _mm_res_ln_kernel(x_ref, w_ref, r_ref, g_ref, b_ref, o_ref):
    y = jnp.dot(x_ref[...].astype(BF), w_ref[...], preferred_element_type=F32)
    o_ref[...] = _layernorm(DN_ALPHA * r_ref[...] + y, g_ref[...], b_ref[...])


def _mm_res_ln(x, w, res, g, b, *, tm, name):
    m, k = x.shape
    n = w.shape[1]
    return pl.pallas_call(
        _mm_res_ln_kernel,
        grid=(pl.cdiv(m, tm),),
        in_specs=[pl.BlockSpec((tm, k), lambda i: (i, 0)),
                  pl.BlockSpec((k, n), lambda i: (0, 0)),
                  pl.BlockSpec((tm, n), lambda i: (i, 0)),
                  pl.BlockSpec((1, n), lambda i: (0, 0)),
                  pl.BlockSpec((1, n), lambda i: (0, 0))],
        out_specs=pl.BlockSpec((tm, n), lambda i: (i, 0)),
        out_shape=jax.ShapeDtypeStruct((m, n), F32),
        compiler_params=_params(1),
        name=name,
    )(x, w, res, g.reshape(1, n), b.reshape(1, n))


def _combine_ln_kernel(*refs):
    ys, (w_ref, r_ref, g_ref, b_ref, o_ref) = refs[:TOP_K], refs[TOP_K:]
    y = ys[0][...] * w_ref[:, 0:1]
    for k in range(1, TOP_K):
        y = y + ys[k][...] * w_ref[:, k:k + 1]
    o_ref[...] = _layernorm(DN_ALPHA * r_ref[...] + y, g_ref[...], b_ref[...])


def _combine_ln(ys, weights, res, g, b, *, tm, name):
    m, n = res.shape
    row_spec = pl.BlockSpec((tm, n), lambda i: (i, 0))
    return pl.pallas_call(
        _combine_ln_kernel,
        grid=(pl.cdiv(m, tm),),
        in_specs=[row_spec] * TOP_K + [pl.BlockSpec((tm, LANES), lambda i: (i, 0)), row_spec,
                                       pl.BlockSpec((1, n), lambda i: (0, 0)),
                                       pl.BlockSpec((1, n), lambda i: (0, 0))],
        out_specs=row_spec,
        out_shape=jax.ShapeDtypeStruct((m, n), F32),
        compiler_params=_params(1),
        name=name,
    )(*ys, weights, res, g.reshape(1, n), b.reshape(1, n))


def _flash_block(chains, carry):
    ss = []
    for k, q, bias, mask, _, _ in chains:
        s = lax.dot_general(k, q, _NT, preferred_element_type=F32) + bias
        ss.append(s if mask is None else jnp.where(mask, s, _NEG))
    stats = []
    for i, (_, _, _, _, shift, _) in enumerate(chains):
        top = jnp.max(ss[i], axis=0, keepdims=True)
        if carry is None:
            m_new, a = top, None
            p = jnp.exp2(ss[i] - m_new)
            l = jnp.sum(p, axis=0, keepdims=True)
        else:
            m, l_old, _ = carry[i]
            m_new = jnp.maximum(m, top - shift)
            a = jnp.exp2(m - m_new)
            p = jnp.exp2(ss[i] - (m_new + shift))
            l = a * l_old + jnp.sum(p, axis=0, keepdims=True)
        stats.append((m_new, l, a, p.astype(BF)))
    out = []
    for i, (_, _, _, _, _, v_t) in enumerate(chains):
        m_new, l, a, p = stats[i]
        pv = jnp.dot(v_t, p, preferred_element_type=F32)
        out.append((m_new, l, pv if carry is None else a * carry[i][2] + pv))
    return tuple(out)


def _tile_geometry():
    key_j = lax.broadcasted_iota(jnp.int32, (KEY_TILE, Q_TILE), 0)
    qry_i = lax.broadcasted_iota(jnp.int32, (KEY_TILE, Q_TILE), 1)
    causal = qry_i >= key_j
    rel = (key_j - (KEY_TILE - 1)).astype(F32)
    return causal, rel


def _stage_kv(k_ref, v_ref, kb_ref, vt_ref, n_blocks):
    kb_ref[...] = k_ref[...].astype(BF)
    for n in range(n_blocks):
        vt_ref[n] = v_ref[n * KEY_TILE:(n + 1) * KEY_TILE, :].T.astype(BF)


def _moba_prompt_kernel(q_ref, k_ref, v_ref, into_ref, o_ref, kb_ref, vt_ref, km_ref, *, n_blocks):
    del into_ref
    cur = pl.program_id(1)

    @pl.when(cur == 0)
    def _():
        _stage_kv(k_ref, v_ref, kb_ref, vt_ref, n_blocks)
        km_ref[...] = jnp.zeros(km_ref.shape, F32)
        for n in range(n_blocks):
            km_ref[n:n + 1, :] = jnp.mean(k_ref[n * MOBA_BLOCK:(n + 1) * MOBA_BLOCK, :],
                                          axis=0, keepdims=True)

    causal, rel = _tile_geometry()
    q = q_ref[...]
    lane_q = lax.broadcasted_iota(jnp.int32, q.shape, 1)
    lane_p = lax.broadcasted_iota(jnp.int32, (Q_TILE, LANES), 1)
    blk_row = lax.broadcasted_iota(jnp.int32, (km_ref.shape[0], Q_TILE), 0)

    consts = []
    for h in range(H_A):
        pair, hh = divmod(h, 2)
        cols = slice(pair * LANES, (pair + 1) * LANES)
        slope2 = SLOPES_A[h] * LOG2E
        gate = lax.dot_general(km_ref[...], jnp.where(lane_q // DH_A == h, q, 0.0), _NT,
                               precision=lax.Precision.HIGHEST,
                               preferred_element_type=F32)
        gate = jnp.where(blk_row < cur, gate, _NEG)
        _, _, sel = _top_picks(gate, blk_row, MOBA_TOPK, axis=0)
        qm = jnp.where(lane_p // DH_A == hh, q[:, cols] * (DH_A ** -0.5 * LOG2E), 0.0).astype(BF)
        consts.append((cols, slope2, sel, qm, slope2 * rel))

    def block(n, carry):
        start = pl.multiple_of(n * KEY_TILE, KEY_TILE)
        chains = []
        for cols, slope2, sel, qm, bias in consts:
            if carry is None:
                mask, shift = causal, None
            else:
                mask = jnp.max(jnp.where(blk_row == n, sel, 0.0), axis=0, keepdims=True) > 0.5
                shift = slope2 * jnp.full((1, Q_TILE), (cur - n) * KEY_TILE, jnp.int32).astype(F32)
            chains.append((kb_ref[pl.ds(start, KEY_TILE), cols], qm, bias, mask, shift,
                           vt_ref[n, cols, :]))
        return _flash_block(chains, carry)

    final = lax.fori_loop(0, cur, block, block(cur, None))
    row = lax.broadcasted_iota(jnp.int32, (LANES, Q_TILE), 0)
    for pair in range(H_A // 2):
        (_, l0, a0), (_, l1, a1) = final[2 * pair], final[2 * pair + 1]
        o_t = jnp.where(row < DH_A, a0 / l0, a1 / l1)
        o_ref[:, pair * LANES:(pair + 1) * LANES] = o_t.T


def _moba_prompt(proj, batch, seq, into):
    n_q = seq // Q_TILE
    n_blocks = seq // MOBA_BLOCK
    km_rows = -(-n_blocks // SUBLANES) * SUBLANES
    cq, ck, cv = COL_QA // MIX_A, COL_KA // MIX_A, COL_VA // MIX_A
    return pl.pallas_call(
        functools.partial(_moba_prompt_kernel, n_blocks=n_blocks),
        grid=(batch, n_q),
        in_specs=[pl.BlockSpec((Q_TILE, MIX_A), lambda b, i: (b * n_q + i, cq)),
                  pl.BlockSpec((seq, MIX_A), lambda b, i: (b, ck)),
                  pl.BlockSpec((seq, MIX_A), lambda b, i: (b, cv)),
                  pl.BlockSpec(memory_space=pl.ANY)],
        out_specs=pl.BlockSpec((Q_TILE, MIX_A), lambda b, i: (b * n_q + i, 0)),
        out_shape=jax.ShapeDtypeStruct(into.shape, F32),
        input_output_aliases={3: 0},
        scratch_shapes=[pltpu.VMEM((seq, MIX_A), BF),
                        pltpu.VMEM((n_blocks, MIX_A, KEY_TILE), BF),
                        pltpu.VMEM((km_rows, MIX_A), F32)],
        compiler_params=_params(2),
        name="moba_prompt",
    )(proj, proj, proj, into)


def _lambda(lq1, lk1, lq2, lk2, lam_init):
    return (jnp.exp(jnp.sum(lq1[...] * lk1[...], axis=-1, keepdims=True))
            - jnp.exp(jnp.sum(lq2[...] * lk2[...], axis=-1, keepdims=True)) + lam_init)


def _diff_prompt_kernel(q_ref, k_ref, v_ref, lq1, lk1, lq2, lk2, ng_ref, into_ref, o_ref, kb_ref,
                        vt_ref, *, lam_init, n_blocks):
    del into_ref
    cur = pl.program_id(1)

    @pl.when(cur == 0)
    def _():
        _stage_kv(k_ref, v_ref, kb_ref, vt_ref, n_blocks)

    causal, rel = _tile_geometry()
    lam = _lambda(lq1, lk1, lq2, lk2, lam_init)
    lane = lax.broadcasted_iota(jnp.int32, (Q_TILE, LANES), 1)

    consts = []
    for h in range(H_C):
        cols = slice(h * LANES, (h + 1) * LANES)
        slope2 = SLOPES_C[h] * LOG2E
        qh = q_ref[:, cols] * (DH_C ** -0.5 * LOG2E)
        consts.append((cols, slope2, jnp.where(lane < DH_C, qh, 0.0).astype(BF),
                       jnp.where(lane >= DH_C, qh, 0.0).astype(BF), slope2 * rel))

    def block(n, carry):
        start = pl.multiple_of(n * KEY_TILE, KEY_TILE)
        chains = []
        for cols, slope2, q1, q2, bias in consts:
            kb = kb_ref[pl.ds(start, KEY_TILE), cols]
            v_t = vt_ref[n, cols, :]
            if carry is None:
                mask, shift = causal, None
            else:
                mask = None
                shift = slope2 * jnp.full((1, Q_TILE), (cur - n) * KEY_TILE, jnp.int32).astype(F32)
            chains += [(kb, q1, bias, mask, shift, v_t), (kb, q2, bias, mask, shift, v_t)]
        return _flash_block(chains, carry)

    final = lax.fori_loop(0, cur, block, block(cur, None))
    for h in range(H_C):
        (_, l1, a1), (_, l2, a2) = final[2 * h], final[2 * h + 1]
        o_t = a1 / l1 - lam * (a2 / l2)
        o_t = o_t * lax.rsqrt(jnp.mean(o_t * o_t, axis=0, keepdims=True) + LN_EPS) * ng_ref[...]
        o_ref[:, h * LANES:(h + 1) * LANES] = (o_t * (1.0 - lam_init)).T


def _diff_prompt(proj, lam_vecs, norm_g, batch, seq, lam_init, into):
    n_q = seq // Q_TILE
    n_blocks = seq // KEY_TILE
    cq, ck, cv = COL_QC // MIX_C, COL_KC // MIX_C, COL_VC // MIX_C
    vec = pl.BlockSpec((1, DH_C), lambda b, i: (0, 0))
    return pl.pallas_call(
        functools.partial(_diff_prompt_kernel, lam_init=lam_init, n_blocks=n_blocks),
        grid=(batch, n_q),
        in_specs=[pl.BlockSpec((Q_TILE, MIX_C), lambda b, i: (b * n_q + i, cq)),
                  pl.BlockSpec((seq, MIX_C), lambda b, i: (b, ck)),
                  pl.BlockSpec((seq, MIX_C), lambda b, i: (b, cv)),
                  vec, vec, vec, vec,
                  pl.BlockSpec((2 * DH_C, 1), lambda b, i: (0, 0)),
                  pl.BlockSpec(memory_space=pl.ANY)],
        out_specs=pl.BlockSpec((Q_TILE, MIX_C), lambda b, i: (b * n_q + i, 0)),
        out_shape=jax.ShapeDtypeStruct(into.shape, F32),
        input_output_aliases={8: 0},
        scratch_shapes=[pltpu.VMEM((seq, MIX_C), BF),
                        pltpu.VMEM((n_blocks, MIX_C, KEY_TILE), BF)],
        compiler_params=_params(2),
        name="diff_prompt",
    )(proj, proj, proj, *lam_vecs, norm_g.reshape(2 * DH_C, 1), into)


def _sample_attn_kernel(pt_ref, qa_ref, ka_ref, va_ref, qc_ref, kc_ref, vc_ref,
                        lq1, lk1, lq2, lk2, ng_ref, mk_hbm, mv_hbm, dk_hbm, dv_hbm, oa_all, oc_all,
                        oa_ref, oc_ref, mk_buf, mv_buf, dk_buf, dv_buf, sems, *, layer, lam_init, t_new):
    del oa_all, oc_all
    n = N_PAGES
    seq = pl.program_id(0)
    slot = seq % 2
    streams = ((mk_hbm, mk_buf), (mv_hbm, mv_buf), (dk_hbm, dk_buf), (dv_hbm, dv_buf))

    def page_copies(s, to_slot):
        copies = []
        for j in range(n):
            page = pt_ref[s * n + j]
            for p, (pool, buf) in enumerate(streams):
                copies.append(pltpu.make_async_copy(pool.at[layer, page], buf.at[to_slot, j],
                                                    sems.at[to_slot, p]))
        return copies

    @pl.when(seq == 0)
    def _():
        for c in page_copies(seq, slot):
            c.start()

    @pl.when(seq + 1 < pl.num_programs(0))
    def _():
        for c in page_copies(seq + 1, 1 - slot):
            c.start()

    for c in page_copies(seq, slot):
        c.wait()

    mk = [mk_buf.at[slot, j] for j in range(n)]
    mv = [mv_buf.at[slot, j] for j in range(n)]
    dk = [dk_buf.at[slot, j] for j in range(n)]
    dv = [dv_buf.at[slot, j] for j in range(n)]
    pad_rows = PAGE_SIZE - t_new

    def pad_new(x):
        return jnp.concatenate([x, jnp.zeros((pad_rows, x.shape[1]), x.dtype)], axis=0)

    def head_rows(ref):
        return jnp.concatenate([ref[pl.ds(h, PAGE_SIZE, stride=H_C), :] for h in range(H_C)],
                               axis=1)

    def raw_scores(q_rows, page_scores, k_new):
        qb = q_rows.astype(BF)
        tiles = [page_scores(qb, j) for j in range(n)]
        tiles.append(lax.dot_general(qb, pad_new(k_new).astype(BF), _NT, preferred_element_type=F32))
        return tiles

    def softmax_tiles(tiles, slope_col, t_col, gate_sel):
        shape = tiles[0].shape
        lane_f = lax.broadcasted_iota(jnp.int32, shape, 1).astype(F32)
        slope = jnp.broadcast_to(slope_col, shape)
        t_row = jnp.broadcast_to(t_col, shape)
        near = slope * (lane_f - t_row)
        out = []
        for j in range(n):
            s = tiles[j] + (near - slope * float(PAST_LEN - j * PAGE_SIZE))
            if gate_sel is not None:
                s = jnp.where(gate_sel[j // (MOBA_BLOCK // PAGE_SIZE)], s, _NEG)
            out.append(s)
        out.append(jnp.where(lane_f <= t_row, tiles[n] + near, _NEG))
        top = out[0]
        for s in out[1:]:
            top = jnp.maximum(top, s)
        m = jnp.max(top, axis=-1, keepdims=True)
        ps = [jnp.exp(s - m) for s in out]
        tot = ps[0]
        for p in ps[1:]:
            tot = tot + p
        return ps, jnp.sum(tot, axis=-1, keepdims=True)

    def fold_heads(o, head_w, n_heads):
        r = lax.broadcasted_iota(jnp.int32, o.shape, 0)
        c = lax.broadcasted_iota(jnp.int32, o.shape, 1)
        o = jnp.where(r // t_new == c // head_w, o, 0.0)
        out = o[0:t_new]
        for h in range(1, n_heads):
            out = out + o[h * t_new:(h + 1) * t_new]
        return out

    def row_const(rows, values):
        h = lax.broadcasted_iota(jnp.int32, (rows, 1), 0) // t_new
        out = jnp.full((rows, 1), values[-1], F32)
        for i in range(len(values) - 2, -1, -1):
            out = jnp.where(h % len(values) == i, values[i], out)
        return out

    ra = H_A * t_new
    qa = qa_ref[...] * (DH_A ** -0.5)
    qa_rows = jnp.concatenate([qa] * H_A, axis=0)
    r = lax.broadcasted_iota(jnp.int32, qa_rows.shape, 0)
    c = lax.broadcasted_iota(jnp.int32, qa_rows.shape, 1)
    qa_rows = jnp.where(r // t_new == c // DH_A, qa_rows, 0.0)
    row_t = (lax.broadcasted_iota(jnp.int32, (ra, 1), 0) % t_new).astype(F32)
    row_slope = row_const(ra, SLOPES_A)
    pages_per_block = MOBA_BLOCK // PAGE_SIZE
    n_past_blocks = PAST_LEN // MOBA_BLOCK

    rc = H_C * t_new
    qc = qc_ref[...] * (DH_C ** -0.5)
    qc_rows = jnp.concatenate([qc] * (2 * H_C), axis=0)
    r = lax.broadcasted_iota(jnp.int32, qc_rows.shape, 0)
    c = lax.broadcasted_iota(jnp.int32, qc_rows.shape, 1)
    keep = ((r % rc) // t_new == c // (2 * DH_C)) & (r // rc == (c % (2 * DH_C)) // DH_C)
    qc_rows = jnp.where(keep, qc_rows, 0.0)
    row_t_c = (lax.broadcasted_iota(jnp.int32, (2 * rc, 1), 0) % t_new).astype(F32)
    row_slope_c = row_const(2 * rc, SLOPES_C)
    lam = _lambda(lq1, lk1, lq2, lk2, lam_init)

    raw_a = raw_scores(qa_rows,
                       lambda qb, j: jnp.dot(qb, mk[j][...].astype(BF), preferred_element_type=F32),
                       ka_ref[...])
    raw_c = raw_scores(qc_rows,
                       lambda qb, j: lax.dot_general(qb, head_rows(dk[j]).astype(BF), _NT,
                                                     preferred_element_type=F32),
                       kc_ref[...])
    lane_g = lax.broadcasted_iota(jnp.int32, (ra, PAGE_SIZE), 1)
    gate = jnp.full((ra, PAGE_SIZE), _NEG, F32)
    for b in range(n_past_blocks):
        tot = raw_a[b * pages_per_block]
        for j in range(1, pages_per_block):
            tot = tot + raw_a[b * pages_per_block + j]
        gate = jnp.where(lane_g == b, jnp.sum(tot, axis=-1, keepdims=True), gate)
    _, _, sel = _top_picks(gate, lane_g, MOBA_TOPK)
    gate_sel = [jnp.broadcast_to(jnp.max(jnp.where(lane_g == b, sel, 0.0), axis=-1, keepdims=True),
                                 (ra, PAGE_SIZE)) > 0.5 for b in range(n_past_blocks)]
    ps_a, l_a = softmax_tiles(raw_a, row_slope, row_t, gate_sel)
    ps_c, l_c = softmax_tiles(raw_c, row_slope_c, row_t_c, None)
    inv1 = jnp.broadcast_to(1.0 / l_c[0:rc], (rc, PAGE_SIZE))
    inv2 = jnp.broadcast_to(lam / l_c[rc:2 * rc], (rc, PAGE_SIZE))
    ws = [p[0:rc] * inv1 - p[rc:2 * rc] * inv2 for p in ps_c]
    oa = jnp.dot(ps_a[n].astype(BF), pad_new(va_ref[...]).astype(BF), preferred_element_type=F32)
    for j in range(n):
        oa = oa + lax.dot_general(ps_a[j].astype(BF), mv[j][...].astype(BF), _NT,
                                  preferred_element_type=F32)
    oc = jnp.dot(ws[n].astype(BF), pad_new(vc_ref[...]).astype(BF), preferred_element_type=F32)
    for j in range(n):
        oc = oc + jnp.dot(ws[j].astype(BF), head_rows(dv[j]).astype(BF),
                          preferred_element_type=F32)
    oa_ref[...] = fold_heads(oa / l_a, DH_A, H_A)
    oc = fold_heads(oc, 2 * DH_C, H_C)
    for h in range(H_C):
        cols = slice(h * LANES, (h + 1) * LANES)
        o = oc[:, cols]
        o = o * lax.rsqrt(jnp.mean(o * o, axis=-1, keepdims=True) + LN_EPS) * ng_ref[...]
        oc_ref[:, cols] = o * (1.0 - lam_init)


def _sample_attn(layer, page_table, qa, ka, va, qc, kc, vc, pools, lam_vecs, norm_g, lam_init,
                 oa_all, oc_all, row0):
    n_seq = page_table.shape[0]
    t_new = qa.shape[0] // n_seq
    pt_flat = page_table.reshape(-1)

    def new_spec(w):
        return pl.BlockSpec((t_new, w), lambda b, pt: (b, 0))

    vec = pl.BlockSpec((1, DH_C), lambda b, pt: (0, 0))
    in_specs = [new_spec(MIX_A)] * 3 + [new_spec(MIX_C)] * 3 + [vec] * 4
    in_specs.append(pl.BlockSpec((1, 2 * DH_C), lambda b, pt: (0, 0)))
    args = [qa, ka, va, qc, kc, vc, *lam_vecs, norm_g.reshape(1, 2 * DH_C)]
    in_specs += [pl.BlockSpec(memory_space=pl.ANY)] * len(pools)
    args += list(pools)
    blk0 = row0 // t_new
    first_alias = 1 + len(args)
    in_specs += [pl.BlockSpec(memory_space=pl.ANY)] * 2
    args += [oa_all, oc_all]
    n_slots = 2
    grid_spec = pltpu.PrefetchScalarGridSpec(
        num_scalar_prefetch=1, grid=(n_seq,), in_specs=in_specs,
        out_specs=[pl.BlockSpec((t_new, MIX_A), lambda b, pt: (blk0 + b, 0)),
                   pl.BlockSpec((t_new, MIX_C), lambda b, pt: (blk0 + b, 0))],
        scratch_shapes=[pltpu.VMEM((n_slots, N_PAGES) + pool.shape[2:], F32) for pool in pools]
        + [pltpu.SemaphoreType.DMA((n_slots, len(pools)))])
    return pl.pallas_call(
        functools.partial(_sample_attn_kernel, layer=layer, lam_init=lam_init, t_new=t_new),
        grid_spec=grid_spec,
        out_shape=[jax.ShapeDtypeStruct(oa_all.shape, F32), jax.ShapeDtypeStruct(oc_all.shape, F32)],
        input_output_aliases={first_alias: 0, first_alias + 1: 1},
        compiler_params=_params(1),
        name="sample_attn",
    )(pt_flat, *args)


def _conv_kernel(ua_ref, ub_ref, hist_ref, cw_ref, cb_ref, g_ref, b_ref, into_ref, o_ref, st_ref, z_ref,
                 *, n_seq, t_len, chunk):
    del into_ref
    pad = hist_ref.shape[1]
    lead = pad - (CONV_W - 1)
    for s in range(n_seq):
        rows = slice(s * t_len, (s + 1) * t_len)
        z_ref[0:pad, :] = hist_ref[s]
        z_ref[pad:pad + t_len, :] = ua_ref[rows, :] * _sigmoid(ub_ref[rows, :])
        for c0 in range(0, t_len, chunk):
            acc = jnp.zeros((chunk, C_B), F32) + cb_ref[...]
            for w in range(CONV_W):
                lo = c0 + lead + w
                acc = acc + z_ref[lo:lo + chunk, :] * cw_ref[w:w + 1, :]
            y = _layernorm(acc, g_ref[...], b_ref[...])
            o_ref[s * t_len + c0:s * t_len + c0 + chunk, :] = y * _sigmoid(y)
        st_ref[s] = z_ref[lead + t_len:pad + t_len, :]


def _conv_module(proj, row0, n_seq_total, t_len, seq_per_step, hist, conv_w, conv_b, ln_g, ln_b,
                 name, into):
    rows = seq_per_step * t_len
    blk0 = row0 // rows
    pad = hist.shape[1]
    vec = pl.BlockSpec((1, C_B), lambda i: (0, 0))
    return pl.pallas_call(
        functools.partial(_conv_kernel, n_seq=seq_per_step, t_len=t_len, chunk=min(t_len, 256)),
        grid=(n_seq_total // seq_per_step,),
        in_specs=[pl.BlockSpec((rows, C_B), lambda i: (blk0 + i, COL_UA // C_B)),
                  pl.BlockSpec((rows, C_B), lambda i: (blk0 + i, COL_UB // C_B)),
                  pl.BlockSpec((seq_per_step, pad, C_B), lambda i: (i, 0, 0)),
                  pl.BlockSpec((CONV_W, C_B), lambda i: (0, 0)),
                  vec, vec, vec, pl.BlockSpec(memory_space=pl.ANY)],
        out_specs=[pl.BlockSpec((rows, C_B), lambda i: (blk0 + i, 0)),
                   pl.BlockSpec((seq_per_step, CONV_W - 1, C_B), lambda i: (i, 0, 0))],
        out_shape=[jax.ShapeDtypeStruct(into.shape, F32),
                   jax.ShapeDtypeStruct((n_seq_total, CONV_W - 1, C_B), F32)],
        input_output_aliases={7: 0},
        scratch_shapes=[pltpu.VMEM((pad + t_len, C_B), F32)],
        compiler_params=_params(1),
        name=name,
    )(proj, proj, hist, conv_w, conv_b.reshape(1, C_B), ln_g.reshape(1, C_B), ln_b.reshape(1, C_B),
      into)


def _merge_kernel(oa_ref, ob_ref, oc_ref, g0_ref, g1_ref, g2_ref, x_ref, wa_ref, wb_ref, wc_ref,
                  wo_ref, g_ref, b_ref, o_ref):
    def branch(o_r, w_r, gate_r):
        y = jnp.dot(o_r[...].astype(BF), w_r[...], preferred_element_type=F32)
        return _sigmoid(gate_r[...]) * y

    merged = branch(oa_ref, wa_ref, g0_ref) + branch(ob_ref, wb_ref, g1_ref) + branch(oc_ref, wc_ref, g2_ref)
    y = jnp.dot(merged.astype(BF), wo_ref[...], preferred_element_type=F32)
    o_ref[...] = _layernorm(DN_ALPHA * x_ref[...] + y, g_ref[...], b_ref[...])


def _merge(oa, ob, oc, proj, x, wa, wb, wc, wo, g, b, *, tm):
    m = x.shape[0]
    d = D_MODEL

    def rows(w, col=0):
        return pl.BlockSpec((tm, w), lambda i, col=col: (i, col))

    def whole(a):
        return pl.BlockSpec(a.shape, lambda i: (0, 0))

    vec = pl.BlockSpec((1, d), lambda i: (0, 0))
    return pl.pallas_call(
        _merge_kernel,
        grid=(pl.cdiv(m, tm),),
        in_specs=[rows(MIX_A), rows(C_B), rows(MIX_C), rows(d, 0), rows(d, 1), rows(d, 2), rows(d),
                  whole(wa), whole(wb), whole(wc), whole(wo), vec, vec],
        out_specs=rows(d),
        out_shape=jax.ShapeDtypeStruct((m, d), F32),
        compiler_params=_params(1),
        name="merge",
    )(oa, ob, oc, proj, proj, proj, x, wa, wb, wc, wo, g.reshape(1, d), b.reshape(1, d))


def _xattn_kernel(q_ref, mk_ref, mv_ref, into_ref, o_ref, *, n_seq, t_len, interleaved):
    del into_ref
    def head(ref, s, h):
        if interleaved:
            tiles = XDH // LANES
            return jnp.concatenate(
                [ref[s, pl.ds(c * XH + h, MEM_LEN, stride=XH * tiles), :] for c in range(tiles)],
                axis=1).astype(BF)
        return ref[s, :, h * XDH:(h + 1) * XDH].astype(BF)

    pairs = [(s, h, slice(s * t_len, (s + 1) * t_len), slice(h * XDH, (h + 1) * XDH))
             for s in range(n_seq) for h in range(XH)]
    scores = [lax.dot_general((q_ref[rows, cols] * (XDH ** -0.5)).astype(BF), head(mk_ref, s, h), _NT,
                              preferred_element_type=F32) for s, h, rows, cols in pairs]
    probs, sums = [], []
    for sc in scores:
        p = jnp.exp(sc - jnp.max(sc, axis=-1, keepdims=True))
        probs.append(p.astype(BF))
        sums.append(jnp.sum(p, axis=-1, keepdims=True))
    outs = [jnp.dot(p, head(mv_ref, s, h), preferred_element_type=F32)
            for p, (s, h, _, _) in zip(probs, pairs)]
    for o, l, (_, _, rows, cols) in zip(outs, sums, pairs):
        o_ref[rows, cols] = o / l


def _xattn(q, row0, n_rows, mem_k, mem_v, layer, seq_per_step, t_len, steps_per_mem, name, into):
    rows = seq_per_step * t_len
    blk0 = row0 // rows
    interleaved = mem_k.ndim == 4
    if interleaved:
        mem_spec = pl.BlockSpec((None, seq_per_step) + mem_k.shape[2:],
                                lambda i: (layer, i // steps_per_mem, 0, 0))
    else:
        mem_spec = pl.BlockSpec((seq_per_step, MEM_LEN, D_MODEL), lambda i: (i // steps_per_mem, 0, 0))
    return pl.pallas_call(
        functools.partial(_xattn_kernel, n_seq=seq_per_step, t_len=t_len, interleaved=interleaved),
        grid=(n_rows // rows,),
        in_specs=[pl.BlockSpec((rows, D_MODEL), lambda i: (blk0 + i, 0)), mem_spec, mem_spec,
                  pl.BlockSpec(memory_space=pl.ANY)],
        out_specs=pl.BlockSpec((rows, D_MODEL), lambda i: (blk0 + i, 0)),
        out_shape=jax.ShapeDtypeStruct(into.shape, F32),
        input_output_aliases={3: 0},
        compiler_params=_params(1),
        name=name,
    )(q, mem_k, mem_v, into)


def _router_kernel(x_ref, w_ref, b_ref, idx_ref, gate_ref):
    logits = jnp.dot(x_ref[...], w_ref[...], precision=lax.Precision.HIGHEST,
                     preferred_element_type=F32) + b_ref[...]
    lane = lax.broadcasted_iota(jnp.int32, logits.shape, 1)
    logits = jnp.where(lane < N_EXPERTS, logits, _NEG)
    vals, idxs, _ = _top_picks(logits, lane, TOP_K)
    es = [jnp.exp(v - vals[0]) for v in vals]
    denom = es[0]
    for e in es[1:]:
        denom = denom + e
    idx_out = jnp.zeros(logits.shape, jnp.int32)
    gate_out = jnp.zeros(logits.shape, F32)
    for k in range(TOP_K):
        idx_out = jnp.where(lane == k, idxs[k].astype(jnp.int32), idx_out)
        gate_out = jnp.where(lane == k, es[k] / denom, gate_out)
    idx_ref[...] = idx_out
    gate_ref[...] = gate_out


def _router(x, w_pad, b_pad, *, tm):
    m, d = x.shape
    return pl.pallas_call(
        _router_kernel,
        grid=(pl.cdiv(m, tm),),
        in_specs=[pl.BlockSpec((tm, d), lambda i: (i, 0)),
                  pl.BlockSpec((d, LANES), lambda i: (0, 0)),
                  pl.BlockSpec((1, LANES), lambda i: (0, 0))],
        out_specs=[pl.BlockSpec((tm, LANES), lambda i: (i, 0))] * 2,
        out_shape=[jax.ShapeDtypeStruct((m, LANES), jnp.int32),
                   jax.ShapeDtypeStruct((m, LANES), F32)],
        compiler_params=_params(1),
        name="router",
    )(x, w_pad, b_pad)


def _experts_kernel(be_ref, nv_ref, first_ref, slot_ref, next_ref, x_ref, wgu_hbm, bgu_ref, wd_hbm, bd_ref,
                    o_ref, wgu_f, wd_f, wgu_b, wd_b, sems, *, layer):
    i = pl.program_id(0)
    live = i < nv_ref[0]

    def weight_copies(e, s):
        return (pltpu.make_async_copy(wgu_hbm.at[layer, e], wgu_f.at[s], sems.at[s, 0]),
                pltpu.make_async_copy(wd_hbm.at[layer, e], wd_f.at[s], sems.at[s, 1]))

    @pl.when(first_ref[i] == 1)
    def _():
        slot = slot_ref[i]

        @pl.when(i == 0)
        def _():
            for c in weight_copies(be_ref[i], slot):
                c.start()

        @pl.when(next_ref[i] >= 0)
        def _():
            for c in weight_copies(next_ref[i], 1 - slot):
                c.start()

        for c in weight_copies(be_ref[i], slot):
            c.wait()
        wgu_b[...] = wgu_f[slot].astype(BF)
        wd_b[...] = wd_f[slot].astype(BF)

    @pl.when(live)
    def _():
        h = jnp.dot(x_ref[...].astype(BF), wgu_b[...], preferred_element_type=F32) + bgu_ref[...]
        hg = jnp.minimum(h[:, :D_FF], SWIGLU_LIMIT)
        hl = jnp.clip(h[:, D_FF:], -SWIGLU_LIMIT, SWIGLU_LIMIT)
        a = hg * _sigmoid(SWIGLU_ALPHA * hg) * (hl + 1.0)
        o_ref[...] = jnp.dot(a.astype(BF), wd_b[...], preferred_element_type=F32) + bd_ref[...]

    @pl.when(jnp.logical_not(live))
    def _():
        o_ref[...] = jnp.zeros(o_ref.shape, F32)


def _experts(layer, blk_e, n_live, xb, w_gate_up, b_gate_up, w_down, b_down):
    cap, d = xb.shape
    n_blocks = cap // MOE_BLOCK
    blk = jnp.arange(n_blocks, dtype=jnp.int32)
    prev_e = jnp.concatenate([jnp.full((1,), -1, jnp.int32), blk_e[:-1]])
    first = (blk < n_live[0]) & (blk_e != prev_e)
    slot = (jnp.cumsum(first.astype(jnp.int32)) - 1) % 2
    first_at = jnp.where(first, blk, n_blocks)
    next_first = jnp.concatenate([lax.cummin(first_at, reverse=True)[1:],
                                  jnp.full((1,), n_blocks, jnp.int32)])
    next_e = jnp.where(next_first < n_blocks, blk_e[jnp.minimum(next_first, n_blocks - 1)], -1)
    n_slots = 2
    hbm = pl.BlockSpec(memory_space=pl.ANY)
    grid_spec = pltpu.PrefetchScalarGridSpec(
        num_scalar_prefetch=5, grid=(n_blocks,),
        in_specs=[pl.BlockSpec((MOE_BLOCK, d), lambda i, be, *_: (i, 0)),
                  hbm,
                  pl.BlockSpec((None, None, 1, 2 * D_FF), lambda i, be, *_: (layer, be[i], 0, 0)),
                  hbm,
                  pl.BlockSpec((None, None, 1, d), lambda i, be, *_: (layer, be[i], 0, 0))],
        out_specs=pl.BlockSpec((MOE_BLOCK, d), lambda i, be, *_: (i, 0)),
        scratch_shapes=[pltpu.VMEM((n_slots, d, 2 * D_FF), F32), pltpu.VMEM((n_slots, D_FF, d), F32),
                        pltpu.VMEM((d, 2 * D_FF), BF), pltpu.VMEM((D_FF, d), BF),
                        pltpu.SemaphoreType.DMA((n_slots, 2))])
    return pl.pallas_call(
        functools.partial(_experts_kernel, layer=layer),
        grid_spec=grid_spec,
        out_shape=jax.ShapeDtypeStruct((cap, d), F32),
        compiler_params=_params(1),
        name="experts",
    )(blk_e, n_live, first.astype(jnp.int32), slot.astype(jnp.int32), next_e.astype(jnp.int32), xb,
      w_gate_up, b_gate_up.reshape(DEPTH, N_EXPERTS, 1, 2 * D_FF),
      w_down, b_down.reshape(DEPTH, N_EXPERTS, 1, d))


def _moe(layer, h, router_w_pad, router_b_pad, w_gate_up, b_gate_up, w_down, b_down, ln_g, ln_b):
    n, d = h.shape
    idx, gate = _router(h, router_w_pad, router_b_pad, tm=512)
    top_i = idx[:, :TOP_K]
    n_flat = n * TOP_K
    key_bits = (n_flat - 1).bit_length()
    flat_e = top_i.reshape(-1)
    flat_i = jnp.arange(n_flat, dtype=jnp.int32)
    sorted_keys = jnp.sort(flat_e * (1 << key_bits) + flat_i)
    st = (sorted_keys & ((1 << key_bits) - 1)) // TOP_K
    chunk = 256
    experts = jnp.arange(N_EXPERTS, dtype=jnp.int32)
    onehot = (flat_e[:, None] == experts[None, :]).reshape(n_flat // chunk, chunk, N_EXPERTS)
    within = jnp.einsum('ij,cjk->cik', jnp.tril(jnp.ones((chunk, chunk), BF)), onehot.astype(BF),
                        preferred_element_type=F32)
    chunk_tot = within[:, -1, :]
    chunk_off = jnp.cumsum(chunk_tot, axis=0) - chunk_tot
    counts = (chunk_off[-1] + chunk_tot[-1]).astype(jnp.int32)
    padded = (counts + MOE_BLOCK - 1) // MOE_BLOCK * MOE_BLOCK
    pad_end = jnp.cumsum(padded)
    pad_start = pad_end - padded
    start = jnp.cumsum(counts) - counts
    pos = jnp.sum(jnp.where(onehot, within + chunk_off[:, None, :] - 1.0
                            + pad_start.astype(F32)[None, None, :], 0.0),
                  axis=-1).astype(jnp.int32).reshape(n, TOP_K)
    n_blocks = n_flat // MOE_BLOCK + N_EXPERTS
    blk_row0 = jnp.arange(n_blocks, dtype=jnp.int32) * MOE_BLOCK
    blk_e = jnp.minimum(jnp.sum((pad_end[None, :] <= blk_row0[:, None]).astype(jnp.int32), axis=1),
                        N_EXPERTS - 1)
    blk_hot = blk_e[:, None] == experts[None, :]
    blk_off = blk_row0 - jnp.sum(jnp.where(blk_hot, pad_start[None, :], 0), axis=1)
    blk_cnt = jnp.sum(jnp.where(blk_hot, counts[None, :], 0), axis=1)
    blk_src = jnp.sum(jnp.where(blk_hot, start[None, :], 0), axis=1) + blk_off
    n_live = (pad_end[-1:] // MOE_BLOCK).astype(jnp.int32)
    lane_row = jnp.arange(MOE_BLOCK, dtype=jnp.int32)[None, :]
    row_ok = blk_off[:, None] + lane_row < blk_cnt[:, None]
    src = jnp.clip(blk_src[:, None] + lane_row, 0, n_flat - 1)
    buf_t = jnp.where(row_ok, st[src.reshape(-1)].reshape(n_blocks, MOE_BLOCK), 0).reshape(-1)
    xb = h[buf_t]
    yb = _experts(layer, blk_e, n_live, xb, w_gate_up, b_gate_up, w_down, b_down)
    return _combine_ln([yb[pos[:, k]] for k in range(TOP_K)], gate, h, ln_g, ln_b, tm=512,
                       name="moe_out")


def kernel(x_prompt, x_sample, mem_prompt, cache_moba_k, cache_moba_v, cache_diff_k, cache_diff_v, cache_mem_k, cache_mem_v, state_conv, page_table, w_in, b_in, conv_w, conv_b, conv_ln_g, conv_ln_b, lam_q1, lam_k1, lam_q2, lam_k2, diff_norm_g, w_branch, w_out, ln1_g, ln1_b, xq_w, xk_w, xv_w, xo_w, ln2_g, ln2_b, router_w, router_b, w_gate_up, b_gate_up, w_down, b_down, ln3_g, ln3_b):
    bp, seq, d = x_prompt.shape
    db, dseq, _ = x_sample.shape
    n_p = bp * seq
    n_s = db * dseq
    n_pool = cache_moba_k.shape[1]
    hist_pad = 32

    o_qa, o_ka, o_va = 0, MIX_A, 2 * MIX_A
    o_u = 3 * MIX_A
    o_qc = o_u + 2 * C_B
    o_kc, o_vc = o_qc + MIX_C, o_qc + 2 * MIX_C
    o_g = o_qc + 3 * MIX_C

    def permute_cols(a):
        return jnp.concatenate([a[..., o_g:], a[..., o_qc:o_g], a[..., o_qa:o_u], a[..., o_u:o_qc]], axis=-1)

    pools = [jnp.transpose(c, (0, 1, 3, 4, 2)).reshape(DEPTH, n_pool, MIX_A, PAGE_SIZE)
             for c in (cache_moba_k, cache_moba_v)]
    pools += [c.reshape(DEPTH, n_pool, PAGE_SIZE * H_C, 2 * DH_C) for c in (cache_diff_k, cache_diff_v)]
    def mem_rows(c):
        c = c.reshape(DEPTH, db, MEM_LEN, XH, XDH // LANES, LANES)
        return jnp.transpose(c, (0, 1, 2, 4, 3, 5)).reshape(DEPTH, db, MEM_LEN * XH * XDH // LANES, LANES)

    mem_k_s = mem_rows(cache_mem_k)
    mem_v_s = mem_rows(cache_mem_v)
    mem2d = mem_prompt.reshape(bp * MEM_LEN, d)
    hist_p = jnp.zeros((bp, hist_pad, C_B), F32)
    hist_s = jnp.pad(state_conv, ((0, 0), (0, 0), (hist_pad - (CONV_W - 1), 0), (0, 0)))
    router_w_pad = jnp.pad(router_w, ((0, 0), (0, 0), (0, LANES - N_EXPERTS)))
    router_b_pad = jnp.pad(router_b, ((0, 0), (0, LANES - N_EXPERTS))).reshape(DEPTH, 1, LANES)

    x = jnp.concatenate([x_prompt.reshape(n_p, d), x_sample.reshape(n_s, d)], axis=0)
    oa, ob, oc, xo = (jnp.zeros((n_p + n_s, w), F32) for w in (MIX_A, C_B, MIX_C, d))
    outs = {k: [] for k in ("ka_p", "va_p", "kc_p", "vc_p", "mk_p", "mv_p", "cs_p",
                            "ka_s", "va_s", "kc_s", "vc_s", "cs_s")}

    for l in range(DEPTH):
        lam_init = 0.8 - 0.6 * math.exp(-0.3 * l)
        lam_vecs = [v[l].reshape(1, DH_C) for v in (lam_q1, lam_k1, lam_q2, lam_k2)]
        w_in_l = permute_cols(w_in[l]).astype(BF)
        b_in_l = permute_cols(b_in[l])
        wa = w_branch[l, :MIX_A].astype(BF)
        wb = w_branch[l, MIX_A:MIX_A + C_B].astype(BF)
        wc = w_branch[l, MIX_A + C_B:].astype(BF)

        proj = _matmul(x, w_in_l, b_in_l, tm=512, tn=IN_W // 2, name="proj_in")
        proj_s = proj[n_p:]
        new = {k: proj_s[:, c:c + w] for k, c, w in
               (("qa", COL_QA, MIX_A), ("ka", COL_KA, MIX_A), ("va", COL_VA, MIX_A),
                ("qc", COL_QC, MIX_C), ("kc", COL_KC, MIX_C), ("vc", COL_VC, MIX_C))}

        oa = _moba_prompt(proj, bp, seq, oa)
        oc = _diff_prompt(proj, lam_vecs, diff_norm_g[l], bp, seq, lam_init, oc)
        oa, oc = _sample_attn(l, page_table, new["qa"], new["ka"], new["va"],
                              new["qc"], new["kc"], new["vc"], pools, lam_vecs,
                              diff_norm_g[l], lam_init, oa, oc, n_p)
        ob, cs_p = _conv_module(proj, 0, bp, seq, 1, hist_p, conv_w[l], conv_b[l],
                                conv_ln_g[l], conv_ln_b[l], "conv_prompt", ob)
        ob, cs_s = _conv_module(proj, n_p, db, dseq, 16, hist_s[l], conv_w[l], conv_b[l],
                                conv_ln_g[l], conv_ln_b[l], "conv_sample", ob)
        h1 = _merge(oa, ob, oc, proj, x, wa, wb, wc, w_out[l].astype(BF), ln1_g[l], ln1_b[l], tm=256)

        mk_p = _matmul(mem2d, xk_w[l].astype(BF), tm=512, tn=d, name="mem_k")
        mv_p = _matmul(mem2d, xv_w[l].astype(BF), tm=512, tn=d, name="mem_v")
        q = _matmul(h1, xq_w[l].astype(BF), tm=1024, tn=d, name="xattn_q")
        xo = _xattn(q, 0, n_p, mk_p.reshape(bp, MEM_LEN, d), mv_p.reshape(bp, MEM_LEN, d),
                    l, 1, 512, seq // 512, "xattn_prompt", xo)
        xo = _xattn(q, n_p, n_s, mem_k_s, mem_v_s, l, 4, dseq, 1, "xattn_sample", xo)
        h2 = _mm_res_ln(xo, xo_w[l].astype(BF), h1, ln2_g[l], ln2_b[l], tm=512, name="xattn_out")

        x = _moe(l, h2, router_w_pad[l], router_b_pad[l], w_gate_up, b_gate_up, w_down, b_down,
                 ln3_g[l], ln3_b[l])

        proj_p = proj[:n_p]
        outs["ka_p"].append(proj_p[:, COL_KA:COL_KA + MIX_A].reshape(bp, seq, H_A, DH_A))
        outs["va_p"].append(proj_p[:, COL_VA:COL_VA + MIX_A].reshape(bp, seq, H_A, DH_A))
        outs["kc_p"].append(proj_p[:, COL_KC:COL_KC + MIX_C].reshape(bp, seq, H_C, 2 * DH_C))
        outs["vc_p"].append(proj_p[:, COL_VC:COL_VC + MIX_C].reshape(bp, seq, H_C, 2 * DH_C))
        outs["mk_p"].append(mk_p.reshape(bp, MEM_LEN, XH, XDH))
        outs["mv_p"].append(mv_p.reshape(bp, MEM_LEN, XH, XDH))
        outs["cs_p"].append(cs_p)
        outs["ka_s"].append(new["ka"].reshape(db, dseq, H_A, DH_A))
        outs["va_s"].append(new["va"].reshape(db, dseq, H_A, DH_A))
        outs["kc_s"].append(new["kc"].reshape(db, dseq, H_C, 2 * DH_C))
        outs["vc_s"].append(new["vc"].reshape(db, dseq, H_C, 2 * DH_C))
        outs["cs_s"].append(cs_s)

    return (x[:n_p].reshape(bp, seq, d), x[n_p:].reshape(db, dseq, d),
            jnp.stack(outs["ka_p"]), jnp.stack(outs["va_p"]), jnp.stack(outs["kc_p"]),
            jnp.stack(outs["vc_p"]), jnp.stack(outs["mk_p"]), jnp.stack(outs["mv_p"]),
            jnp.stack(outs["cs_p"]), jnp.stack(outs["ka_s"]), jnp.stack(outs["va_s"]),
            jnp.stack(outs["kc_s"]), jnp.stack(outs["vc_s"]), jnp.stack(outs["cs_s"]))
```
